```python
import math
import jax
import jax.numpy as jnp
from jax import lax
import numpy as np

D_MODEL = 4096
BATCH = 2
SEQ = 4096
DEPTH = 1
DEC_BATCH = 128
DEC_SEQ = 8
PAST_LEN = 2048
PAGE_SIZE = 128

MIX_WIDTH = D_MODEL
DIFF_WIDTH = MIX_WIDTH // 2
SB_WIDTH = MIX_WIDTH - DIFF_WIDTH
DIFF_HEAD_DIM = 128
DIFF_HEADS = DIFF_WIDTH // (2 * DIFF_HEAD_DIM)
SB_HEAD_DIM = 128
SB_HEADS = SB_WIDTH // SB_HEAD_DIM
IN_COLS = 3 * DIFF_WIDTH + 3 * SB_WIDTH
MEM_LEN = 256
MEM_HEADS = 4
MEM_HEAD_DIM = D_MODEL // MEM_HEADS
D_FF = 4 * D_MODEL
N_BUCKETS = 32
MAX_DISTANCE = 128
Q_BLOCK = 128
LN_EPS = 1e-5
RMS_EPS = 1e-5
DEEPNORM_ALPHA = (2.0 * DEPTH) ** 0.25
DEEPNORM_BETA = (8.0 * DEPTH) ** -0.25
NEG_INF = -1e30
F32 = jnp.float32

kernel_name = 'hymba_diffattn_stickbreak_deepnorm_decode_step'


def layer_norm(x, g, b):
    xf = x.astype(F32)
    mu = jnp.mean(xf, -1, keepdims=True)
    var = jnp.mean(jnp.square(xf - mu), -1, keepdims=True)
    return ((xf - mu) * lax.rsqrt(var + LN_EPS) * g.astype(F32) + b.astype(F32)).astype(x.dtype)


def t5_bucket(dist):
    max_exact = N_BUCKETS // 2
    d = jnp.maximum(dist, 0)
    df = jnp.maximum(d, 1).astype(F32)
    large = max_exact + (jnp.log(df / max_exact) / math.log(MAX_DISTANCE / max_exact)
                         * (N_BUCKETS - max_exact)).astype(jnp.int32)
    large = jnp.minimum(large, N_BUCKETS - 1)
    return jnp.where(d < max_exact, d, large)


def relative_bias(q_pos, k_pos, rel_bias):
    bucket = t5_bucket(q_pos[:, None] - k_pos[None, :])
    return jnp.transpose(rel_bias[bucket].astype(F32), (2, 0, 1))


def diff_lambda(lq1, lk1, lq2, lk2, lam_init):
    return (jnp.exp(jnp.sum(lq1.astype(F32) * lk1.astype(F32)))
            - jnp.exp(jnp.sum(lq2.astype(F32) * lk2.astype(F32))) + lam_init)


def differential_attention(q, k, v, bias, mask, lam, lam_init, subln_w):
    s = jnp.einsum('bqhcd,bkhcd->bhcqk', q, k, preferred_element_type=F32) * (DIFF_HEAD_DIM ** -0.5)
    s = jnp.where(mask, s + bias[:, None], NEG_INF)
    p = jax.nn.softmax(s, axis=-1)
    a = p[:, :, 0] - lam * p[:, :, 1]
    o = jnp.einsum('bhqk,bkhe->bqhe', a.astype(v.dtype), v, preferred_element_type=F32)
    o = o * lax.rsqrt(jnp.mean(o * o, -1, keepdims=True) + RMS_EPS) * subln_w.astype(F32) * (1.0 - lam_init)
    return o.astype(v.dtype)


def stick_breaking_attention(q, k, v, mask):
    z = jnp.einsum('bqhd,bkhd->bhqk', q, k, preferred_element_type=F32) * (SB_HEAD_DIM ** -0.5)
    log_beta = jax.nn.log_sigmoid(z)
    log_one_minus = jnp.where(mask, jax.nn.log_sigmoid(-z), 0.0)
    later = lax.cumsum(log_one_minus, axis=3, reverse=True) - log_one_minus
    a = jnp.where(mask, jnp.exp(log_beta + later), 0.0)
    o = jnp.einsum('bhqk,bkhd->bqhd', a.astype(v.dtype), v, preferred_element_type=F32)
    return o.astype(v.dtype)


def split_heads(h):
    B, T = h.shape[0], h.shape[1]
    dw, sw = DIFF_WIDTH, SB_WIDTH
    q_d = h[..., :dw].reshape(B, T, DIFF_HEADS, 2, DIFF_HEAD_DIM)
    k_d = h[..., dw:2 * dw].reshape(B, T, DIFF_HEADS, 2, DIFF_HEAD_DIM)
    v_d = h[..., 2 * dw:3 * dw].reshape(B, T, DIFF_HEADS, 2 * DIFF_HEAD_DIM)
    o = 3 * dw
    q_s = h[..., o:o + sw].reshape(B, T, SB_HEADS, SB_HEAD_DIM)
    k_s = h[..., o + sw:o + 2 * sw].reshape(B, T, SB_HEADS, SB_HEAD_DIM)
    v_s = h[..., o + 2 * sw:o + 3 * sw].reshape(B, T, SB_HEADS, SB_HEAD_DIM)
    return q_d, k_d, v_d, q_s, k_s, v_s


def head_group_mixers(q_d, q_s, k_d, v_d, k_s, v_s, q_pos, k_pos, lam, lam_init, subln_w, rel_bias):
    causal = k_pos[None, :] <= q_pos[:, None]
    strict = k_pos[None, :] < q_pos[:, None]
    bias = relative_bias(q_pos, k_pos, rel_bias)
    o_d = differential_attention(q_d, k_d, v_d, bias, causal, lam, lam_init, subln_w)
    o_s = stick_breaking_attention(q_s, k_s, v_s, strict)
    B, Tq = q_d.shape[0], q_d.shape[1]
    return jnp.concatenate([o_d.reshape(B, Tq, DIFF_WIDTH), o_s.reshape(B, Tq, SB_WIDTH)], axis=-1)


def memory_kv(mem, w_ck, w_cv):
    B = mem.shape[0]
    mk = (mem @ w_ck).reshape(B, MEM_LEN, MEM_HEADS, MEM_HEAD_DIM)
    mv = (mem @ w_cv).reshape(B, MEM_LEN, MEM_HEADS, MEM_HEAD_DIM)
    return mk, mv


def cross_attention(x, mk, mv, w_cq, w_co):
    B, T = x.shape[0], x.shape[1]
    q = (x @ w_cq).reshape(B, T, MEM_HEADS, MEM_HEAD_DIM)
    s = jnp.einsum('bqhd,bkhd->bhqk', q, mk, preferred_element_type=F32) * (MEM_HEAD_DIM ** -0.5)
    p = jax.nn.softmax(s, axis=-1)
    o = jnp.einsum('bhqk,bkhd->bqhd', p.astype(mv.dtype), mv, preferred_element_type=F32).astype(x.dtype)
    return o.reshape(B, T, D_MODEL) @ w_co


def squared_relu_mlp(x, w_up, b_up, w_down, b_down):
    h = jnp.square(jax.nn.relu(x @ w_up + b_up))
    return h @ w_down + b_down


def to_blocks(t, n_blk):
    return jnp.swapaxes(t.reshape((t.shape[0], n_blk, Q_BLOCK) + t.shape[2:]), 0, 1)


def from_blocks(t):
    t = jnp.swapaxes(t, 0, 1)
    return t.reshape((t.shape[0], t.shape[1] * t.shape[2]) + t.shape[3:])


def setup_inputs(seed: int = 0) -> dict:
    key = jax.random.key(seed)
    ks = jax.random.split(key, 40)
    n_pages = PAST_LEN // PAGE_SIZE
    n_used = DEC_BATCH * n_pages
    n_pool = n_used + max(1, n_used // 4)

    def nrm(k, shape, scale):
        return jax.random.normal(k, shape, F32) * scale

    d_in = D_MODEL ** -0.5
    col_scale = jnp.concatenate([
        jnp.ones((2 * DIFF_WIDTH,), F32), jnp.full((DIFF_WIDTH,), DEEPNORM_BETA, F32),
        jnp.ones((2 * SB_WIDTH,), F32), jnp.full((SB_WIDTH,), DEEPNORM_BETA, F32)])
    perm = jax.random.permutation(ks[1], n_pool)
    page_table = perm[:n_used].reshape(DEC_BATCH, n_pages).astype(jnp.int32)
    return {
        'x_prompt': nrm(ks[2], (BATCH, SEQ, D_MODEL), 1.0),
        'x_sample': nrm(ks[3], (DEC_BATCH, DEC_SEQ, D_MODEL), 1.0),
        'mem_prompt': nrm(ks[4], (BATCH, MEM_LEN, D_MODEL), 1.0),
        'cache_diff_k': nrm(ks[5], (DEPTH, n_pool, PAGE_SIZE, DIFF_HEADS, 2, DIFF_HEAD_DIM), 1.0),
        'cache_diff_v': nrm(ks[6], (DEPTH, n_pool, PAGE_SIZE, DIFF_HEADS, 2 * DIFF_HEAD_DIM), DEEPNORM_BETA),
        'cache_sb_k': nrm(ks[7], (DEPTH, n_pool, PAGE_SIZE, SB_HEADS, SB_HEAD_DIM), 1.0),
        'cache_sb_v': nrm(ks[8], (DEPTH, n_pool, PAGE_SIZE, SB_HEADS, SB_HEAD_DIM), DEEPNORM_BETA),
        'cache_mem_k': nrm(ks[9], (DEPTH, DEC_BATCH, MEM_LEN, MEM_HEADS, MEM_HEAD_DIM), 1.0),
        'cache_mem_v': nrm(ks[10], (DEPTH, DEC_BATCH, MEM_LEN, MEM_HEADS, MEM_HEAD_DIM), DEEPNORM_BETA),
        'page_table': page_table,
        'w_in': nrm(ks[0], (DEPTH, D_MODEL, IN_COLS), d_in) * col_scale,
        'lambda_q1': nrm(ks[11], (DEPTH, DIFF_HEAD_DIM), 0.1),
        'lambda_k1': nrm(ks[12], (DEPTH, DIFF_HEAD_DIM), 0.1),
        'lambda_q2': nrm(ks[13], (DEPTH, DIFF_HEAD_DIM), 0.1),
        'lambda_k2': nrm(ks[14], (DEPTH, DIFF_HEAD_DIM), 0.1),
        'subln_w': 1.0 + nrm(ks[15], (DEPTH, 2 * DIFF_HEAD_DIM), 0.02),
        'rel_bias': nrm(ks[16], (N_BUCKETS, DIFF_HEADS), 0.5),
        'w_out': nrm(ks[17], (DEPTH, MIX_WIDTH, D_MODEL), MIX_WIDTH ** -0.5 * DEEPNORM_BETA),
        'ln1_g': 1.0 + nrm(ks[18], (DEPTH, D_MODEL), 0.02),
        'ln1_b': nrm(ks[19], (DEPTH, D_MODEL), 0.01),
        'w_cq': nrm(ks[20], (DEPTH, D_MODEL, D_MODEL), d_in),
        'w_ck': nrm(ks[21], (DEPTH, D_MODEL, D_MODEL), d_in),
        'w_cv': nrm(ks[22], (DEPTH, D_MODEL, D_MODEL), d_in * DEEPNORM_BETA),
        'w_co': nrm(ks[23], (DEPTH, D_MODEL, D_MODEL), d_in * DEEPNORM_BETA),
        'ln2_g': 1.0 + nrm(ks[24], (DEPTH, D_MODEL), 0.02),
        'ln2_b': nrm(ks[25], (DEPTH, D_MODEL), 0.01),
        'w_up': nrm(ks[26], (DEPTH, D_MODEL, D_FF), d_in),
        'b_up': nrm(ks[27], (DEPTH, D_FF), 0.01),
        'w_down': nrm(ks[28], (DEPTH, D_FF, D_MODEL), D_FF ** -0.5 * DEEPNORM_BETA),
        'b_down': nrm(ks[29], (DEPTH, D_MODEL), 0.01),
        'ln3_g': 1.0 + nrm(ks[30], (DEPTH, D_MODEL), 0.02),
        'ln3_b': nrm(ks[31], (DEPTH, D_MODEL), 0.01),
    }


def reference(x_prompt, x_sample, mem_prompt, cache_diff_k, cache_diff_v, cache_sb_k, cache_sb_v,
              cache_mem_k, cache_mem_v, page_table, w_in, lambda_q1, lambda_k1, lambda_q2, lambda_k2,
              subln_w, rel_bias, w_out, ln1_g, ln1_b, w_cq, w_ck, w_cv, w_co, ln2_g, ln2_b,
              w_up, b_up, w_down, b_down, ln3_g, ln3_b):
    n_pages = PAST_LEN // PAGE_SIZE
    n_blk = SEQ // Q_BLOCK
    seq_pos = jnp.arange(SEQ, dtype=jnp.int32)
    s_qpos = PAST_LEN + jnp.arange(DEC_SEQ, dtype=jnp.int32)
    s_kpos = jnp.arange(PAST_LEN + DEC_SEQ, dtype=jnp.int32)

    p_kd, p_vd, p_ks, p_vs, p_mk, p_mv = [], [], [], [], [], []
    s_kd, s_vd, s_ks, s_vs = [], [], [], []
    xp = x_prompt
    xs = x_sample
    for l in range(DEPTH):
        lam_init = 0.8 - 0.6 * math.exp(-0.3 * l)
        lam = diff_lambda(lambda_q1[l], lambda_k1[l], lambda_q2[l], lambda_k2[l], lam_init)

        q_d, k_d, v_d, q_s, k_s, v_s = split_heads(xp @ w_in[l])

        def prompt_block(args):
            qb_d, qb_s, i = args
            q_pos = i * Q_BLOCK + jnp.arange(Q_BLOCK, dtype=jnp.int32)
            return head_group_mixers(qb_d, qb_s, k_d, v_d, k_s, v_s, q_pos, seq_pos,
                                     lam, lam_init, subln_w[l], rel_bias)

        mixed = from_blocks(lax.map(prompt_block, (to_blocks(q_d, n_blk), to_blocks(q_s, n_blk),
                                                   jnp.arange(n_blk, dtype=jnp.int32))))
        xp = layer_norm(DEEPNORM_ALPHA * xp + mixed @ w_out[l], ln1_g[l], ln1_b[l])
        mk, mv = memory_kv(mem_prompt, w_ck[l], w_cv[l])
        xp = layer_norm(DEEPNORM_ALPHA * xp + cross_attention(xp, mk, mv, w_cq[l], w_co[l]), ln2_g[l], ln2_b[l])
        xp = layer_norm(DEEPNORM_ALPHA * xp + squared_relu_mlp(xp, w_up[l], b_up[l], w_down[l], b_down[l]),
                        ln3_g[l], ln3_b[l])
        p_kd.append(k_d); p_vd.append(v_d); p_ks.append(k_s); p_vs.append(v_s)
        p_mk.append(mk); p_mv.append(mv)

        sq_d, sk_d, sv_d, sq_s, sk_s, sv_s = split_heads(xs @ w_in[l])

        def gather(pool):
            rows = pool[page_table]
            return rows.reshape((DEC_BATCH, n_pages * PAGE_SIZE) + rows.shape[3:])

        kd_all = jnp.concatenate([gather(cache_diff_k[l]), sk_d], axis=1)
        vd_all = jnp.concatenate([gather(cache_diff_v[l]), sv_d], axis=1)
        ks_all = jnp.concatenate([gather(cache_sb_k[l]), sk_s], axis=1)
        vs_all = jnp.concatenate([gather(cache_sb_v[l]), sv_s], axis=1)
        mixed_s = head_group_mixers(sq_d, sq_s, kd_all, vd_all, ks_all, vs_all, s_qpos, s_kpos,
                                    lam, lam_init, subln_w[l], rel_bias)
        xs = layer_norm(DEEPNORM_ALPHA * xs + mixed_s @ w_out[l], ln1_g[l], ln1_b[l])
        xs = layer_norm(DEEPNORM_ALPHA * xs + cross_attention(xs, cache_mem_k[l], cache_mem_v[l], w_cq[l], w_co[l]),
                        ln2_g[l], ln2_b[l])
        xs = layer_norm(DEEPNORM_ALPHA * xs + squared_relu_mlp(xs, w_up[l], b_up[l], w_down[l], b_down[l]),
                        ln3_g[l], ln3_b[l])
        s_kd.append(sk_d); s_vd.append(sv_d); s_ks.append(sk_s); s_vs.append(sv_s)

    return (xp, xs,
            jnp.stack(p_kd), jnp.stack(p_vd), jnp.stack(p_ks), jnp.stack(p_vs),
            jnp.stack(p_mk), jnp.stack(p_mv),
            jnp.stack(s_kd), jnp.stack(s_vd), jnp.stack(s_ks), jnp.stack(s_vs))
```

```python
import functools
import math

import numpy as np
import jax
import jax.numpy as jnp
from jax import lax
from jax.experimental import pallas as pl
from jax.experimental.pallas import tpu as pltpu

F32 = jnp.float32
BF16 = jnp.bfloat16

DIFF_HEAD_DIM = 128
DIFF_V_DIM = 2 * DIFF_HEAD_DIM
SB_HEAD_DIM = 128
MEM_HEADS = 4
N_BUCKETS = 32
MAX_DISTANCE = 128
LN_EPS = 1e-5
RMS_EPS = 1e-5
NEG_INF = -1e30
LOG2E = math.log2(math.e)

V7X_LANES = 128
V7X_VMEM_BYTES = 64 * 1024 * 1024
VMEM_LIMIT_BYTES = V7X_VMEM_BYTES - 8 * 1024 * 1024


def _cparams(semantics):
    return pltpu.CompilerParams(dimension_semantics=semantics, vmem_limit_bytes=VMEM_LIMIT_BYTES)


def _nt_dot(a, b):
    return lax.dot_general(a, b, (((1,), (1,)), ((), ())), preferred_element_type=F32)


def _dot(a, b):
    return jnp.dot(a, b, preferred_element_type=F32)


def _mm_kernel(*refs, nk, has_bias, has_res, act, alpha, out_scale, n_out):
    x_ref, w_ref = refs[0], refs[1]
    pos = 2
    bias_ref = res_ref = None
    if has_bias:
        bias_ref = refs[pos]
        pos += 1
    if has_res:
        res_ref = refs[pos]
        pos += 1
    out_refs = refs[pos:pos + n_out]
    acc_ref = refs[pos + n_out] if nk > 1 else None

    def epilogue(y):
        if has_bias:
            y = y + bias_ref[...]
        if act == "relu2":
            y = jnp.square(jnp.maximum(y, 0.0))
        if has_res:
            y = alpha * res_ref[...] + y
        if out_scale != 1.0:
            y = y * out_scale
        for o_ref in out_refs:
            o_ref[...] = y.astype(o_ref.dtype)

    part = _dot(x_ref[...], w_ref[...])
    if nk == 1:
        epilogue(part)
    else:
        k = pl.program_id(2)

        @pl.when(k == 0)
        def _():
            acc_ref[...] = part

        @pl.when(jnp.logical_and(k > 0, k < nk - 1))
        def _():
            acc_ref[...] += part

        @pl.when(k == nk - 1)
        def _():
            epilogue(acc_ref[...] + part)


def _pick_tile(n, pref):
    t = min(n, pref)
    while n % t:
        t //= 2
    return t


def _matmul(x, w, *, out_dtypes, bias=None, res=None, act=None, alpha=1.0, out_scale=1.0,
            tm=1024, tn=1024, tk=4096, name="mm"):
    m, kdim = x.shape
    n = w.shape[1]
    tm, tn, tk = _pick_tile(m, tm), _pick_tile(n, tn), _pick_tile(kdim, tk)
    nk = kdim // tk
    in_specs = [pl.BlockSpec((tm, tk), lambda i, j, k: (i, k)),
                pl.BlockSpec((tk, tn), lambda i, j, k: (k, j))]
    args = [x, w]
    if bias is not None:
        in_specs.append(pl.BlockSpec((1, tn), lambda i, j, k: (0, j)))
        args.append(bias.reshape(1, n).astype(F32))
    if res is not None:
        in_specs.append(pl.BlockSpec((tm, tn), lambda i, j, k: (i, j)))
        args.append(res)
    out_specs = [pl.BlockSpec((tm, tn), lambda i, j, k: (i, j)) for _ in out_dtypes]
    out_shape = [jax.ShapeDtypeStruct((m, n), dt) for dt in out_dtypes]
    scratch = [pltpu.VMEM((tm, tn), F32)] if nk > 1 else []
    kern = functools.partial(_mm_kernel, nk=nk, has_bias=bias is not None, has_res=res is not None,
                             act=act, alpha=alpha, out_scale=out_scale, n_out=len(out_dtypes))
    return pl.pallas_call(
        kern, grid=(m // tm, n // tn, nk), in_specs=in_specs, out_specs=out_specs,
        out_shape=out_shape, scratch_shapes=scratch,
        compiler_params=_cparams(("parallel", "parallel", "arbitrary")), name=name,
    )(*args)


def _ln_kernel(y_ref, g_ref, b_ref, o32_ref, o16_ref):
    y = y_ref[...]
    mu = jnp.mean(y, axis=-1, keepdims=True)
    d = y - mu
    var = jnp.mean(jnp.square(d), axis=-1, keepdims=True)
    o = d * lax.rsqrt(var + LN_EPS) * g_ref[...] + b_ref[...]
    o32_ref[...] = o
    o16_ref[...] = o.astype(BF16)


def _layer_norm(y, g, b, *, tm=256, name="ln"):
    m, d = y.shape
    tm = _pick_tile(m, tm)
    row = pl.BlockSpec((tm, d), lambda i: (i, 0))
    vec = pl.BlockSpec((1, d), lambda i: (0, 0))
    return pl.pallas_call(
        _ln_kernel, grid=(m // tm,), in_specs=[row, vec, vec], out_specs=[row, row],
        out_shape=[jax.ShapeDtypeStruct((m, d), F32), jax.ShapeDtypeStruct((m, d), BF16)],
        compiler_params=_cparams(("parallel",)), name=name,
    )(y, g.reshape(1, d).astype(F32), b.reshape(1, d).astype(F32))


def _bucket_table(n):
    max_exact = N_BUCKETS // 2
    d = np.arange(n)
    df = np.maximum(d, 1).astype(np.float32)
    ratio = np.log(df / np.float32(max_exact)) / np.float32(math.log(MAX_DISTANCE / max_exact))
    large = max_exact + (ratio * np.float32(N_BUCKETS - max_exact)).astype(np.int32)
    large = np.minimum(large, N_BUCKETS - 1)
    return np.where(d < max_exact, d, large).astype(np.int32)


def _near_bias_log2(rel_bias):
    tab = rel_bias.astype(F32)[_bucket_table(MAX_DISTANCE)]
    far = rel_bias.astype(F32)[N_BUCKETS - 1]
    return jnp.transpose(tab - far[None, :]) * LOG2E


def _lambda_value(lq1_ref, lk1_ref, lq2_ref, lk2_ref, lam_init):
    s1 = jnp.sum(lq1_ref[...] * lk1_ref[...], axis=1, keepdims=True)
    s2 = jnp.sum(lq2_ref[...] * lk2_ref[...], axis=1, keepdims=True)
    return jnp.exp(s1) - jnp.exp(s2) + lam_init


def _diff_attn_kernel(lq1_ref, lk1_ref, lq2_ref, lk2_ref, vtab_ref, subln_ref, q_ref, k_ref, v_ref,
                      o_ref, bdiag_ref, bsub_ref, m_ref, l_ref, acc_ref, *, tq, lam_init):
    qi = pl.program_id(2)
    dh = DIFF_HEAD_DIM
    nb = tq // V7X_LANES

    @pl.when(qi == 0)
    def _build_bias_tiles():
        toep = pltpu.roll(jnp.broadcast_to(vtab_ref[...], (V7X_LANES, V7X_LANES)), 0, 1,
                          stride=1, stride_axis=0)
        ii = lax.broadcasted_iota(jnp.int32, (V7X_LANES, V7X_LANES), 0)
        jj = lax.broadcasted_iota(jnp.int32, (V7X_LANES, V7X_LANES), 1)
        on_diag = jnp.where(ii >= jj, toep, NEG_INF)
        below = jnp.where(jj > ii, toep, 0.0)
        zeros = jnp.zeros((V7X_LANES, V7X_LANES), F32)
        masked = jnp.full((V7X_LANES, V7X_LANES), NEG_INF, F32)
        for rb in range(nb):
            for cb in range(nb):
                rows = slice(rb * V7X_LANES, (rb + 1) * V7X_LANES)
                cols = slice(cb * V7X_LANES, (cb + 1) * V7X_LANES)
                if rb == cb:
                    tile = on_diag
                elif rb == cb + 1:
                    tile = below
                elif rb > cb:
                    tile = zeros
                else:
                    tile = masked
                bdiag_ref[rows, cols] = tile
                bsub_ref[rows, cols] = below if (rb == 0 and cb == nb - 1) else zeros

    q = q_ref[...]

    def scores(c, kc):
        return _nt_dot(q[:, c * dh:(c + 1) * dh], kc[:, c * dh:(c + 1) * dh])

    def kv_chunk(j):
        start = pl.multiple_of(j * tq, tq)
        return k_ref[pl.ds(start, tq), :], v_ref[pl.ds(start, tq), :]

    def update(c, s, vc):
        m_prev = m_ref[c]
        m_new = jnp.maximum(m_prev, jnp.max(s, axis=1, keepdims=True))
        scale = jnp.exp2(m_prev - m_new)
        p = jnp.exp2(s - m_new)
        l_ref[c] = scale * l_ref[c] + jnp.sum(p, axis=1, keepdims=True)
        acc_ref[c] = scale * acc_ref[c] + _dot(p.astype(BF16), vc)
        m_ref[c] = m_new

    kc, vc = kv_chunk(qi)
    for c in range(2):
        s = scores(c, kc) + bdiag_ref[...]
        m = jnp.max(s, axis=1, keepdims=True)
        p = jnp.exp2(s - m)
        m_ref[c] = m
        l_ref[c] = jnp.sum(p, axis=1, keepdims=True)
        acc_ref[c] = _dot(p.astype(BF16), vc)

    @pl.when(qi >= 1)
    def _sub_diagonal():
        kc1, vc1 = kv_chunk(qi - 1)
        for c in range(2):
            update(c, scores(c, kc1) + bsub_ref[...], vc1)

    def far_body(j, carry):
        kcj, vcj = kv_chunk(j)
        for c in range(2):
            update(c, scores(c, kcj), vcj)
        return carry

    lax.fori_loop(0, jnp.maximum(qi - 1, 0), far_body, 0)

    lam = _lambda_value(lq1_ref, lk1_ref, lq2_ref, lk2_ref, lam_init)
    o = acc_ref[0] / l_ref[0] - lam * (acc_ref[1] / l_ref[1])
    o = o * lax.rsqrt(jnp.mean(o * o, axis=1, keepdims=True) + RMS_EPS) * subln_ref[...] * (1.0 - lam_init)
    o_ref[...] = o.astype(o_ref.dtype)


def _diff_attention_prompt(qd, kd, vd, lam_vecs, vtab, subln_w, *, lam_init, tq=512):
    b, t, w = qd.shape
    h = w // DIFF_V_DIM
    tq = _pick_tile(t, tq)
    vec = pl.BlockSpec((1, DIFF_HEAD_DIM), lambda bi, hi, qi: (0, 0))
    in_specs = [vec, vec, vec, vec,
                pl.BlockSpec((None, 1, V7X_LANES), lambda bi, hi, qi: (hi, 0, 0)),
                pl.BlockSpec((1, DIFF_V_DIM), lambda bi, hi, qi: (0, 0)),
                pl.BlockSpec((None, tq, DIFF_V_DIM), lambda bi, hi, qi: (bi, qi, hi)),
                pl.BlockSpec((None, t, DIFF_V_DIM), lambda bi, hi, qi: (bi, 0, hi)),
                pl.BlockSpec((None, t, DIFF_V_DIM), lambda bi, hi, qi: (bi, 0, hi))]
    kern = functools.partial(_diff_attn_kernel, tq=tq, lam_init=lam_init)
    return pl.pallas_call(
        kern, grid=(b, h, t // tq), in_specs=in_specs,
        out_specs=pl.BlockSpec((None, tq, DIFF_V_DIM), lambda bi, hi, qi: (bi, qi, hi)),
        out_shape=jax.ShapeDtypeStruct((b, t, w), BF16),
        scratch_shapes=[pltpu.VMEM((tq, tq), F32), pltpu.VMEM((tq, tq), F32),
                        pltpu.VMEM((2, tq, 1), F32), pltpu.VMEM((2, tq, 1), F32),
                        pltpu.VMEM((2, tq, DIFF_V_DIM), F32)],
        compiler_params=_cparams(("arbitrary", "arbitrary", "arbitrary")), name="diff_attn_prompt",
    )(*lam_vecs, vtab.reshape(h, 1, V7X_LANES), subln_w.reshape(1, DIFF_V_DIM).astype(F32), qd, kd, vd)


def _later_matrix(n):
    u = (np.arange(n)[:, None] > np.arange(n)[None, :]).astype(np.float32)
    return jnp.asarray(np.concatenate([u, u], axis=0), dtype=BF16)


def _stick_break_terms(z):
    sp = jnp.log2(1.0 + jnp.exp2(-jnp.abs(z)))
    return jnp.minimum(z, 0.0) - sp, jnp.minimum(-z, 0.0) - sp


def _split_bf16(x):
    hi = x.astype(BF16)
    lo = (x - hi.astype(F32)).astype(BF16)
    return hi, lo


def _sb_attn_kernel(u2_ref, q_ref, k_ref, v_ref, o_ref, c_ref, acc_ref, *, tq, tk):
    qi = pl.program_id(2)
    n_sub = tq // tk
    q = q_ref[...]
    c_ref[...] = jnp.zeros_like(c_ref)
    acc_ref[...] = jnp.zeros_like(acc_ref)

    def chunk(start, diag_offset):
        kc = k_ref[pl.ds(start, tk), :]
        vc = v_ref[pl.ds(start, tk), :]
        z = _nt_dot(q, kc)
        log_beta, log_rest = _stick_break_terms(z)
        if diag_offset is not None:
            ii = lax.broadcasted_iota(jnp.int32, (tq, tk), 0)
            jj = lax.broadcasted_iota(jnp.int32, (tq, tk), 1)
            mask = (jj + diag_offset) < ii
            log_rest = jnp.where(mask, log_rest, 0.0)
        hi, lo = _split_bf16(log_rest)
        later = _dot(jnp.concatenate([hi, lo], axis=1), u2_ref[...]) + c_ref[...]
        a = jnp.exp2(log_beta + later)
        if diag_offset is not None:
            a = jnp.where(mask, a, 0.0)
        acc_ref[...] += _dot(a.astype(BF16), vc)
        c_ref[...] += jnp.sum(log_rest, axis=1, keepdims=True)

    for r in reversed(range(n_sub)):
        chunk(pl.multiple_of(qi * tq + r * tk, tk), r * tk)

    n_far = qi * n_sub

    def far_body(j, carry):
        chunk(pl.multiple_of((n_far - 1 - j) * tk, tk), None)
        return carry

    lax.fori_loop(0, n_far, far_body, 0)
    o_ref[...] = acc_ref[...].astype(o_ref.dtype)


def _sb_attention_prompt(qs, ks, vs, *, tq=512, tk=256):
    b, t, w = qs.shape
    h = w // SB_HEAD_DIM
    tq = _pick_tile(t, tq)
    tk = _pick_tile(tq, tk)
    in_specs = [pl.BlockSpec((2 * tk, tk), lambda bi, hi, qi: (0, 0)),
                pl.BlockSpec((None, tq, SB_HEAD_DIM), lambda bi, hi, qi: (bi, qi, hi)),
                pl.BlockSpec((None, t, SB_HEAD_DIM), lambda bi, hi, qi: (bi, 0, hi)),
                pl.BlockSpec((None, t, SB_HEAD_DIM), lambda bi, hi, qi: (bi, 0, hi))]
    kern = functools.partial(_sb_attn_kernel, tq=tq, tk=tk)
    return pl.pallas_call(
        kern, grid=(b, h, t // tq), in_specs=in_specs,
        out_specs=pl.BlockSpec((None, tq, SB_HEAD_DIM), lambda bi, hi, qi: (bi, qi, hi)),
        out_shape=jax.ShapeDtypeStruct((b, t, w), BF16),
        scratch_shapes=[pltpu.VMEM((tq, 1), F32), pltpu.VMEM((tq, SB_HEAD_DIM), F32)],
        compiler_params=_cparams(("parallel", "parallel", "parallel")), name="sb_attn_prompt",
    )(_later_matrix(tk), qs, ks, vs)


def _expand_queries(q, n_groups, group_cols):
    r, c = q.shape
    tiled = jnp.tile(q, (n_groups, 1))
    row_g = lax.broadcasted_iota(jnp.int32, (n_groups * r, c), 0) >> _log2(r)
    col_g = lax.broadcasted_iota(jnp.int32, (n_groups * r, c), 1) >> _log2(group_cols)
    return jnp.where(row_g == col_g, tiled, 0.0).astype(BF16)


def _log2(n):
    assert n & (n - 1) == 0
    return n.bit_length() - 1


def _sample_attn_kernel(pt_ref, lq1_ref, lk1_ref, lq2_ref, lk2_ref, subln_ref, btail_ref, ut_ref,
                        qd_ref, qs_ref, knd_ref, kns_ref, vnd_ref, vns_ref, *rest,
                        pages_per_step, n_pages, n_q, n_new, lam_init):
    del pt_ref
    pps = pages_per_step
    kd_refs, ks_refs = rest[0:pps], rest[pps:2 * pps]
    vd_refs, vs_refs = rest[2 * pps:3 * pps], rest[3 * pps:4 * pps]
    od_ref, os_ref, qexp_ref, s_ref, a_ref, acc_ref = rest[4 * pps:]
    step = pl.program_id(1)
    n_steps = n_pages // pps
    page = V7X_LANES
    n_past = n_pages * page
    lanes = V7X_LANES
    pair = 2 * n_q
    n_pair = lanes // pair

    def pad_to_page(x):
        return jnp.concatenate([x, jnp.zeros((page - n_new, x.shape[1]), x.dtype)], axis=0)

    @pl.when(step == 0)
    def _new_token_scores():
        qexp_ref[0] = _expand_queries(qd_ref[...], lanes // n_q, DIFF_HEAD_DIM)
        qexp_ref[1] = _expand_queries(qs_ref[...], lanes // n_q, SB_HEAD_DIM)
        s_ref[0, n_past:, :] = _nt_dot(knd_ref[...], qexp_ref[0])
        s_ref[1, n_past:, :] = _nt_dot(kns_ref[...], qexp_ref[1])

    @pl.when(step < n_steps)
    def _key_phase():
        for p in range(pps):
            row = pl.multiple_of((step * pps + p) * page, page)
            s_ref[0, pl.ds(row, page), :] = _nt_dot(kd_refs[p][...].astype(BF16), qexp_ref[0])
            s_ref[1, pl.ds(row, page), :] = _nt_dot(ks_refs[p][...].astype(BF16), qexp_ref[1])

    @pl.when(step == n_steps)
    def _attention_weights():
        lane = lax.broadcasted_iota(jnp.int32, (1, lanes), 1)
        lam = _lambda_value(lq1_ref, lk1_ref, lq2_ref, lk2_ref, lam_init)
        s_ref[0, n_past - page:, :] = s_ref[0, n_past - page:, :] + btail_ref[...]
        s = s_ref[0]
        m = jnp.max(s, axis=0, keepdims=True)
        p = jnp.exp2(s - m)
        is_first = ((lane >> _log2(n_q)) & 1) == 0
        coef = jnp.where(is_first, 1.0, -lam) / jnp.sum(p, axis=0, keepdims=True)
        for pg in range(n_pages):
            a_ref[0, pg] = jnp.transpose(p[pg * page:(pg + 1) * page] * coef).astype(BF16)
        a_ref[0, n_pages] = jnp.transpose(pad_to_page(p[n_past:] * coef)).astype(BF16)

        log_beta, log_rest = _stick_break_terms(s_ref[1])
        row = lax.broadcasted_iota(jnp.int32, (page, lanes), 0)
        new_ok = row < (lax.broadcasted_iota(jnp.int32, (page, lanes), 1) & (n_q - 1))
        ut = ut_ref[...]
        rest_new = jnp.where(new_ok, pad_to_page(log_rest[n_past:]), 0.0)
        hi, lo = _split_bf16(rest_new)
        later = _dot(ut, hi) + _dot(ut, lo)
        a_new = jnp.where(new_ok, jnp.exp2(pad_to_page(log_beta[n_past:]) + later), 0.0)
        a_ref[1, n_pages] = jnp.transpose(a_new).astype(BF16)
        carry = jnp.sum(rest_new, axis=0, keepdims=True)
        for pg in reversed(range(n_pages)):
            rows = slice(pg * page, (pg + 1) * page)
            hi, lo = _split_bf16(log_rest[rows])
            later = _dot(ut, hi) + _dot(ut, lo) + carry
            a_ref[1, pg] = jnp.transpose(jnp.exp2(log_beta[rows] + later)).astype(BF16)
            carry = carry + jnp.sum(log_rest[rows], axis=0, keepdims=True)

        for g, vn_ref in ((0, vnd_ref), (1, vns_ref)):
            a_new_page = a_ref[g, n_pages]
            vn = pad_to_page(vn_ref[...])
            for hp in range(n_pair):
                acc_ref[g, hp] = _dot(a_new_page[hp * pair:(hp + 1) * pair, :],
                                      vn[:, hp * DIFF_V_DIM:(hp + 1) * DIFF_V_DIM])

    @pl.when(step >= n_steps)
    def _value_phase():
        for p in range(pps):
            pg = (step - n_steps) * pps + p
            for g, v_refs in ((0, vd_refs), (1, vs_refs)):
                a_pg = a_ref[g, pg]
                vp = v_refs[p][...].astype(BF16)
                for hp in range(n_pair):
                    acc_ref[g, hp] += _dot(a_pg[hp * pair:(hp + 1) * pair, :],
                                           vp[:, hp * DIFF_V_DIM:(hp + 1) * DIFF_V_DIM])

    @pl.when(step == 2 * n_steps - 1)
    def _finish():
        for hp in range(n_pair):
            o = acc_ref[0, hp, :n_q, :] + acc_ref[0, hp, n_q:, :]
            o = (o * lax.rsqrt(jnp.mean(o * o, axis=1, keepdims=True) + RMS_EPS)
                 * subln_ref[...] * (1.0 - lam_init))
            od_ref[:, hp * DIFF_V_DIM:(hp + 1) * DIFF_V_DIM] = o.astype(od_ref.dtype)
            lo_cols = slice(2 * hp * SB_HEAD_DIM, (2 * hp + 1) * SB_HEAD_DIM)
            hi_cols = slice((2 * hp + 1) * SB_HEAD_DIM, (2 * hp + 2) * SB_HEAD_DIM)
            os_ref[:, lo_cols] = acc_ref[1, hp, :n_q, :SB_HEAD_DIM].astype(os_ref.dtype)
            os_ref[:, hi_cols] = acc_ref[1, hp, n_q:, SB_HEAD_DIM:].astype(os_ref.dtype)


def _sample_tail_bias(rel_bias, n_q, n_new, past_len):
    page = V7X_LANES
    key_pos = np.arange(past_len - page, past_len + n_new)[:, None]
    lane = np.arange(V7X_LANES)[None, :]
    head = lane // (2 * n_q)
    q_pos = past_len + lane % n_q
    dist = q_pos - key_pos
    valid = (dist >= 0) & (key_pos < past_len + n_q)
    bucket = _bucket_table(MAX_DISTANCE + n_q + page)[np.clip(dist, 0, None)]
    rb = rel_bias.astype(F32)
    val = (rb[bucket, head] - rb[N_BUCKETS - 1][head]) * LOG2E
    return jnp.where(valid, val, NEG_INF)


def _sample_attention(qd, qs, knd, kns, vnd, vns, cache_kd, cache_ks, cache_vd, cache_vs, page_table,
                      lam_vecs, subln_w, rel_bias, *, lam_init, pages_per_step=2):
    bsz, n_q, w = qd.shape
    n_pages = page_table.shape[1]
    page = cache_kd.shape[1]
    assert page == V7X_LANES and w == 8 * DIFF_V_DIM and n_q == 8
    n_new = 2 * n_q
    pps = pages_per_step
    n_steps = n_pages // pps
    past_len = n_pages * page

    def pad_new(x):
        return jnp.pad(x, ((0, 0), (0, n_new - n_q), (0, 0)))

    def fixed(shape):
        return pl.BlockSpec(shape, lambda b, s, pt: (0,) * len(shape))

    def per_seq(rows):
        return pl.BlockSpec((None, rows, w), lambda b, s, pt: (b, 0, 0))

    def key_page(p):
        return pl.BlockSpec((None, page, w),
                            lambda b, s, pt: (pt[b, jnp.minimum(s, n_steps - 1) * pps + p], 0, 0))

    def value_page(p):
        return pl.BlockSpec((None, page, w),
                            lambda b, s, pt: (pt[b, jnp.maximum(s - n_steps, 0) * pps + p], 0, 0))

    vec = fixed((1, DIFF_HEAD_DIM))
    in_specs = ([vec, vec, vec, vec, fixed((1, DIFF_V_DIM)), fixed((page + n_new, V7X_LANES)),
                 fixed((page, page)), per_seq(n_q), per_seq(n_q),
                 per_seq(n_new), per_seq(n_new), per_seq(n_new), per_seq(n_new)]
                + [key_page(p) for p in range(pps)] * 2 + [value_page(p) for p in range(pps)] * 2)
    ut = jnp.asarray((np.arange(page)[None, :] > np.arange(page)[:, None]).astype(np.float32), dtype=BF16)
    args = ([*lam_vecs, subln_w.reshape(1, DIFF_V_DIM).astype(F32),
             _sample_tail_bias(rel_bias, n_q, n_new, past_len), ut,
             qd, qs, pad_new(knd), pad_new(kns), pad_new(vnd), pad_new(vns)]
            + [cache_kd] * pps + [cache_ks] * pps + [cache_vd] * pps + [cache_vs] * pps)
    kern = functools.partial(_sample_attn_kernel, pages_per_step=pps, n_pages=n_pages, n_q=n_q,
                             n_new=n_new, lam_init=lam_init)
    n_keys = past_len + n_new
    grid_spec = pltpu.PrefetchScalarGridSpec(
        num_scalar_prefetch=1, grid=(bsz, 2 * n_steps), in_specs=in_specs,
        out_specs=[pl.BlockSpec((None, n_q, w), lambda b, s, pt: (b, 0, 0))] * 2,
        scratch_shapes=[pltpu.VMEM((2, V7X_LANES, w), BF16),
                        pltpu.VMEM((2, n_keys, V7X_LANES), F32),
                        pltpu.VMEM((2, n_pages + 1, V7X_LANES, page), BF16),
                        pltpu.VMEM((2, V7X_LANES // (2 * n_q), 2 * n_q, DIFF_V_DIM), F32)])
    return pl.pallas_call(
        kern, grid_spec=grid_spec,
        out_shape=[jax.ShapeDtypeStruct((bsz, n_q, w), F32)] * 2,
        compiler_params=_cparams(("arbitrary", "arbitrary")), name="sample_attn",
    )(page_table, *args)


def _cross_attn_prompt_kernel(q_ref, k_ref, v_ref, o_ref):
    s = _nt_dot(q_ref[...], k_ref[...])
    p = jnp.exp2(s - jnp.max(s, axis=1, keepdims=True))
    o = _dot(p.astype(BF16), v_ref[...]) / jnp.sum(p, axis=1, keepdims=True)
    o_ref[...] = o.astype(o_ref.dtype)


def _cross_attention_prompt(q, mk, mv, *, tq=1024):
    b, t, d = q.shape
    mem_len = mk.shape[1]
    hd = d // MEM_HEADS
    tq = _pick_tile(t, tq)
    q_spec = pl.BlockSpec((None, tq, hd), lambda bi, qi, hi: (bi, qi, hi))
    m_spec = pl.BlockSpec((None, mem_len, hd), lambda bi, qi, hi: (bi, 0, hi))
    return pl.pallas_call(
        _cross_attn_prompt_kernel, grid=(b, t // tq, MEM_HEADS), in_specs=[q_spec, m_spec, m_spec],
        out_specs=q_spec, out_shape=jax.ShapeDtypeStruct((b, t, d), BF16),
        compiler_params=_cparams(("parallel", "parallel", "parallel")), name="cross_attn_prompt",
    )(q, mk, mv)


def _cross_attn_sample_kernel(q_ref, k_ref, v_ref, o_ref, *, n_q):
    hd = q_ref.shape[1] // MEM_HEADS
    qexp = _expand_queries(q_ref[...], MEM_HEADS, hd)
    s = _nt_dot(k_ref[...].astype(BF16), qexp)
    p = jnp.exp2(s - jnp.max(s, axis=0, keepdims=True))
    p = jnp.transpose(p / jnp.sum(p, axis=0, keepdims=True))
    v = v_ref[...].astype(BF16)
    for h in range(MEM_HEADS):
        o = _dot(p[h * n_q:(h + 1) * n_q].astype(BF16), v[:, h * hd:(h + 1) * hd])
        o_ref[:, h * hd:(h + 1) * hd] = o.astype(o_ref.dtype)


def _cross_attention_sample(q, mem_k, mem_v):
    b, n_q, d = q.shape
    mem_len = mem_k.shape[1]
    q_spec = pl.BlockSpec((None, n_q, d), lambda bi: (bi, 0, 0))
    m_spec = pl.BlockSpec((None, mem_len, d), lambda bi: (bi, 0, 0))
    return pl.pallas_call(
        functools.partial(_cross_attn_sample_kernel, n_q=n_q), grid=(b,),
        in_specs=[q_spec, m_spec, m_spec], out_specs=q_spec,
        out_shape=jax.ShapeDtypeStruct((b, n_q, d), F32),
        compiler_params=_cparams(("parallel",)), name="cross_attn_sample",
    )(q, mem_k, mem_v)


def _post_attention(x, mixed, cross_fn, wts, alpha, q_dtype):
    (w_out, ln1_g, ln1_b, w_cq, w_co, ln2_g, ln2_b, w_up, b_up, w_down, b_down, ln3_g, ln3_b) = wts
    d = x.shape[1]
    (y,) = _matmul(mixed, w_out, out_dtypes=[F32], res=x, alpha=alpha, name="mm_out")
    x1, x1b = _layer_norm(y, ln1_g, ln1_b, name="ln1")
    (qc,) = _matmul(x1b, w_cq, out_dtypes=[q_dtype], out_scale=(d // MEM_HEADS) ** -0.5 * LOG2E, name="mm_cq")
    oc = cross_fn(qc).astype(BF16)
    (y,) = _matmul(oc, w_co, out_dtypes=[F32], res=x1, alpha=alpha, name="mm_co")
    x2, x2b = _layer_norm(y, ln2_g, ln2_b, name="ln2")
    (hid,) = _matmul(x2b, w_up, out_dtypes=[BF16], bias=b_up, act="relu2", name="mm_up")
    (y,) = _matmul(hid, w_down, out_dtypes=[F32], bias=b_down, res=x2, alpha=alpha, tk=2048, name="mm_down")
    x3, _ = _layer_norm(y, ln3_g, ln3_b, name="ln3")
    return x3


def _project_qkv(xb, w_sections, q_scale, q_dtype):
    w_qd, w_kd, w_vd, w_qs, w_ks, w_vs = w_sections
    (qd,) = _matmul(xb, w_qd, out_dtypes=[q_dtype], out_scale=q_scale, name="mm_qd")
    kd32, kd = _matmul(xb, w_kd, out_dtypes=[F32, BF16], name="mm_kd")
    vd32, vd = _matmul(xb, w_vd, out_dtypes=[F32, BF16], name="mm_vd")
    (qs,) = _matmul(xb, w_qs, out_dtypes=[q_dtype], out_scale=q_scale, name="mm_qs")
    ks32, ks = _matmul(xb, w_ks, out_dtypes=[F32, BF16], name="mm_ks")
    vs32, vs = _matmul(xb, w_vs, out_dtypes=[F32, BF16], name="mm_vs")
    return (qd, kd, vd, qs, ks, vs), (kd32, vd32, ks32, vs32)


def kernel(x_prompt, x_sample, mem_prompt, cache_diff_k, cache_diff_v, cache_sb_k, cache_sb_v, cache_mem_k, cache_mem_v, page_table, w_in, lambda_q1, lambda_k1, lambda_q2, lambda_k2, subln_w, rel_bias, w_out, ln1_g, ln1_b, w_cq, w_ck, w_cv, w_co, ln2_g, ln2_b, w_up, b_up, w_down, b_down, ln3_g, ln3_b):
    depth = w_in.shape[0]
    assert depth == 1, "single-layer step"
    bsz, seq, d = x_prompt.shape
    dec_b, dec_t, _ = x_sample.shape
    mem_len = mem_prompt.shape[1]
    n_pool, page = cache_diff_k.shape[1], cache_diff_k.shape[2]
    dw = d // 2
    alpha = (2.0 * depth) ** 0.25
    l = 0
    lam_init = 0.8 - 0.6 * math.exp(-0.3 * l)
    q_scale = DIFF_HEAD_DIM ** -0.5 * LOG2E
    assert DIFF_HEAD_DIM == SB_HEAD_DIM

    w_in_b = w_in[l].astype(BF16)
    w_sections = [w_in_b[:, i * dw:(i + 1) * dw] for i in range(6)]
    tail_w = (w_out[l].astype(BF16), ln1_g[l], ln1_b[l], w_cq[l].astype(BF16), w_co[l].astype(BF16),
              ln2_g[l], ln2_b[l], w_up[l].astype(BF16), b_up[l], w_down[l].astype(BF16), b_down[l],
              ln3_g[l], ln3_b[l])
    lam_vecs = [v[l].reshape(1, DIFF_HEAD_DIM).astype(F32) for v in (lambda_q1, lambda_k1, lambda_q2, lambda_k2)]
    vnear = _near_bias_log2(rel_bias)
    vtab = jnp.concatenate([vnear[:, :1], vnear[:, :0:-1]], axis=1)

    xp = x_prompt.reshape(bsz * seq, d)
    (qd, kd, vd, qs, ks, vs), (kd32, vd32, ks32, vs32) = _project_qkv(xp.astype(BF16), w_sections, q_scale,
                                                                      BF16)
    shape3 = (bsz, seq, dw)
    mixed_d = _diff_attention_prompt(qd.reshape(shape3), kd.reshape(shape3), vd.reshape(shape3), lam_vecs,
                                     vtab, subln_w[l], lam_init=lam_init)
    mixed_s = _sb_attention_prompt(qs.reshape(shape3), ks.reshape(shape3), vs.reshape(shape3))
    mixed = jnp.concatenate([mixed_d, mixed_s], axis=-1).reshape(bsz * seq, d)
    memb = mem_prompt.reshape(bsz * mem_len, d).astype(BF16)
    mk32, mk = _matmul(memb, w_ck[l].astype(BF16), out_dtypes=[F32, BF16], name="mm_ck")
    mv32, mv = _matmul(memb, w_cv[l].astype(BF16), out_dtypes=[F32, BF16], name="mm_cv")

    def cross_prompt(qc):
        o = _cross_attention_prompt(qc.reshape(bsz, seq, d), mk.reshape(bsz, mem_len, d),
                                    mv.reshape(bsz, mem_len, d))
        return o.reshape(bsz * seq, d)

    y_prompt = _post_attention(xp, mixed, cross_prompt, tail_w, alpha, BF16).reshape(bsz, seq, d)

    xs = x_sample.reshape(dec_b * dec_t, d)
    (sqd, skd, svd, sqs, sks, svs), (skd32, svd32, sks32, svs32) = _project_qkv(xs.astype(BF16), w_sections,
                                                                              q_scale, F32)
    s3 = (dec_b, dec_t, dw)
    smix_d, smix_s = _sample_attention(
        sqd.reshape(s3), sqs.reshape(s3), skd.reshape(s3), sks.reshape(s3), svd.reshape(s3), svs.reshape(s3),
        cache_diff_k[l].reshape(n_pool, page, dw), cache_sb_k[l].reshape(n_pool, page, dw),
        cache_diff_v[l].reshape(n_pool, page, dw), cache_sb_v[l].reshape(n_pool, page, dw),
        page_table, lam_vecs, subln_w[l], rel_bias, lam_init=lam_init)
    smixed = jnp.concatenate([smix_d, smix_s], axis=-1).reshape(dec_b * dec_t, d).astype(BF16)

    def cross_sample(qc):
        o = _cross_attention_sample(qc.reshape(dec_b, dec_t, d),
                                    cache_mem_k[l].reshape(dec_b, mem_len, d),
                                    cache_mem_v[l].reshape(dec_b, mem_len, d))
        return o.reshape(dec_b * dec_t, d)

    y_sample = _post_attention(xs, smixed, cross_sample, tail_w, alpha, F32).reshape(dec_b, dec_t, d)

    dh, hd = DIFF_HEAD_DIM, dw // DIFF_V_DIM
    hs = dw // SB_HEAD_DIM
    mh = d // MEM_HEADS
    return (y_prompt, y_sample,
            kd32.reshape(1, bsz, seq, hd, 2, dh), vd32.reshape(1, bsz, seq, hd, 2 * dh),
            ks32.reshape(1, bsz, seq, hs, SB_HEAD_DIM), vs32.reshape(1, bsz, seq, hs, SB_HEAD_DIM),
            mk32.reshape(1, bsz, mem_len, MEM_HEADS, mh), mv32.reshape(1, bsz, mem_len, MEM_HEADS, mh),
            skd32.reshape(1, dec_b, dec_t, hd, 2, dh), svd32.reshape(1, dec_b, dec_t, hd, 2 * dh),
            sks32.reshape(1, dec_b, dec_t, hs, SB_HEAD_DIM), svs32.reshape(1, dec_b, dec_t, hs, SB_HEAD_DIM))
```

```python
import functools
import math

import numpy as np
import jax
import jax.numpy as jnp
from jax import lax
from jax.experimental import pallas as pl
from jax.experimental.pallas import tpu as pltpu

F32 = jnp.float32
BF16 = jnp.bfloat16

DIFF_HEAD_DIM = 128
DIFF_V_DIM = 2 * DIFF_HEAD_DIM
SB_HEAD_DIM = 128
MEM_HEADS = 4
N_BUCKETS = 32
MAX_DISTANCE = 128
LN_EPS = 1e-5
RMS_EPS = 1e-5
NEG_INF = -1e30
LOG2E = math.log2(math.e)

V7X_LANES = 128
V7X_VMEM_BYTES = 64 * 1024 * 1024
VMEM_LIMIT_BYTES = V7X_VMEM_BYTES - 8 * 1024 * 1024


def _cparams(semantics):
    return pltpu.CompilerParams(dimension_semantics=semantics, vmem_limit_bytes=VMEM_LIMIT_BYTES)


def _nt_dot(a, b):
    return lax.dot_general(a, b, (((1,), (1,)), ((), ())), preferred_element_type=F32)


def _dot(a, b):
    return jnp.dot(a, b, preferred_element_type=F32)


def _mm_kernel(*refs, nk, has_bias, has_res, act, alpha, out_scale, n_out):
    x_ref, w_ref = refs[0], refs[1]
    pos = 2
    bias_ref = res_ref = None
    if has_bias:
        bias_ref = refs[pos]
        pos += 1
    if has_res:
        res_ref = refs[pos]
        pos += 1
    out_refs = refs[pos:pos + n_out]
    acc_ref = refs[pos + n_out] if nk > 1 else None

    def epilogue(y):
        if has_bias:
            y = y + bias_ref[...]
        if act == "relu2":
            y = jnp.square(jnp.maximum(y, 0.0))
        if has_res:
            y = alpha * res_ref[...] + y
        if out_scale != 1.0:
            y = y * out_scale
        for o_ref in out_refs:
            o_ref[...] = y.astype(o_ref.dtype)

    part = _dot(x_ref[...], w_ref[...])
    if nk == 1:
        epilogue(part)
    else:
        k = pl.program_id(2)

        @pl.when(k == 0)
        def _():
            acc_ref[...] = part

        @pl.when(jnp.logical_and(k > 0, k < nk - 1))
        def _():
            acc_ref[...] += part

        @pl.when(k == nk - 1)
        def _():
            epilogue(acc_ref[...] + part)


def _pick_tile(n, pref):
    t = min(n, pref)
    while n % t:
        t //= 2
    return t


def _matmul(x, w, *, out_dtypes, bias=None, res=None, act=None, alpha=1.0, out_scale=1.0,
            tm=1024, tn=1024, tk=4096, cols=None, name="mm"):
    m, kdim = x.shape
    col0, n = cols if cols is not None else (0, w.shape[1])
    tm, tn, tk = _pick_tile(m, tm), _pick_tile(n, tn), _pick_tile(kdim, tk)
    assert col0 % tn == 0
    j0 = col0 // tn
    nk = kdim // tk
    in_specs = [pl.BlockSpec((tm, tk), lambda i, j, k: (i, k)),
                pl.BlockSpec((tk, tn), lambda i, j, k: (k, j + j0))]
    args = [x, w]
    if bias is not None:
        in_specs.append(pl.BlockSpec((1, tn), lambda i, j, k: (0, j)))
        args.append(bias.reshape(1, n).astype(F32))
    if res is not None:
        in_specs.append(pl.BlockSpec((tm, tn), lambda i, j, k: (i, j)))
        args.append(res)
    out_specs = [pl.BlockSpec((tm, tn), lambda i, j, k: (i, j)) for _ in out_dtypes]
    out_shape = [jax.ShapeDtypeStruct((m, n), dt) for dt in out_dtypes]
    scratch = [pltpu.VMEM((tm, tn), F32)] if nk > 1 else []
    kern = functools.partial(_mm_kernel, nk=nk, has_bias=bias is not None, has_res=res is not None,
                             act=act, alpha=alpha, out_scale=out_scale, n_out=len(out_dtypes))
    return pl.pallas_call(
        kern, grid=(m // tm, n // tn, nk), in_specs=in_specs, out_specs=out_specs,
        out_shape=out_shape, scratch_shapes=scratch,
        compiler_params=_cparams(("parallel", "parallel", "arbitrary")), name=name,
    )(*args)


def _ln_kernel(y_ref, g_ref, b_ref, o32_ref, o16_ref):
    y = y_ref[...]
    mu = jnp.mean(y, axis=-1, keepdims=True)
    d = y - mu
    var = jnp.mean(jnp.square(d), axis=-1, keepdims=True)
    o = d * lax.rsqrt(var + LN_EPS) * g_ref[...] + b_ref[...]
    o32_ref[...] = o
    o16_ref[...] = o.astype(BF16)


def _layer_norm(y, g, b, *, tm=256, name="ln"):
    m, d = y.shape
    tm = _pick_tile(m, tm)
    row = pl.BlockSpec((tm, d), lambda i: (i, 0))
    vec = pl.BlockSpec((1, d), lambda i: (0, 0))
    return pl.pallas_call(
        _ln_kernel, grid=(m // tm,), in_specs=[row, vec, vec], out_specs=[row, row],
        out_shape=[jax.ShapeDtypeStruct((m, d), F32), jax.ShapeDtypeStruct((m, d), BF16)],
        compiler_params=_cparams(("parallel",)), name=name,
    )(y, g.reshape(1, d).astype(F32), b.reshape(1, d).astype(F32))


def _bucket_table(n):
    max_exact = N_BUCKETS // 2
    d = np.arange(n)
    df = np.maximum(d, 1).astype(np.float32)
    ratio = np.log(df / np.float32(max_exact)) / np.float32(math.log(MAX_DISTANCE / max_exact))
    large = max_exact + (ratio * np.float32(N_BUCKETS - max_exact)).astype(np.int32)
    large = np.minimum(large, N_BUCKETS - 1)
    return np.where(d < max_exact, d, large).astype(np.int32)


def _near_bias_log2(rel_bias):
    tab = rel_bias.astype(F32)[_bucket_table(MAX_DISTANCE)]
    far = rel_bias.astype(F32)[N_BUCKETS - 1]
    return jnp.transpose(tab - far[None, :]) * LOG2E


def _lambda_value(lq1_ref, lk1_ref, lq2_ref, lk2_ref, lam_init):
    s1 = jnp.sum(lq1_ref[...] * lk1_ref[...], axis=1, keepdims=True)
    s2 = jnp.sum(lq2_ref[...] * lk2_ref[...], axis=1, keepdims=True)
    return jnp.exp(s1) - jnp.exp(s2) + lam_init


def _diff_attn_kernel(lq1_ref, lk1_ref, lq2_ref, lk2_ref, vtab_ref, subln_ref, q_ref, k_ref, v_ref,
                      o_ref, bdiag_ref, bsub_ref, m_ref, l_ref, acc_ref, *, tq, lam_init):
    qi = pl.program_id(2)
    dh = DIFF_HEAD_DIM
    nb = tq // V7X_LANES

    @pl.when(qi == 0)
    def _build_bias_tiles():
        toep = pltpu.roll(jnp.broadcast_to(vtab_ref[...], (V7X_LANES, V7X_LANES)), 0, 1,
                          stride=1, stride_axis=0)
        ii = lax.broadcasted_iota(jnp.int32, (V7X_LANES, V7X_LANES), 0)
        jj = lax.broadcasted_iota(jnp.int32, (V7X_LANES, V7X_LANES), 1)
        on_diag = jnp.where(ii >= jj, toep, NEG_INF)
        below = jnp.where(jj > ii, toep, 0.0)
        zeros = jnp.zeros((V7X_LANES, V7X_LANES), F32)
        masked = jnp.full((V7X_LANES, V7X_LANES), NEG_INF, F32)
        for rb in range(nb):
            for cb in range(nb):
                rows = slice(rb * V7X_LANES, (rb + 1) * V7X_LANES)
                cols = slice(cb * V7X_LANES, (cb + 1) * V7X_LANES)
                if rb == cb:
                    tile = on_diag
                elif rb == cb + 1:
                    tile = below
                elif rb > cb:
                    tile = zeros
                else:
                    tile = masked
                bdiag_ref[rows, cols] = tile
                bsub_ref[rows, cols] = below if (rb == 0 and cb == nb - 1) else zeros

    q = q_ref[...]

    def scores(c, kc):
        return _nt_dot(q[:, c * dh:(c + 1) * dh], kc[:, c * dh:(c + 1) * dh])

    def kv_chunk(j):
        start = pl.multiple_of(j * tq, tq)
        return k_ref[pl.ds(start, tq), :], v_ref[pl.ds(start, tq), :]

    n_rep_s, n_rep_v = tq // V7X_LANES, DIFF_V_DIM // V7X_LANES

    def update(c, s, vc):
        m_prev = m_ref[c]
        m_new = jnp.maximum(m_prev, jnp.max(s, axis=1, keepdims=True))
        scale = jnp.exp2(m_prev - m_new)
        p = jnp.exp2(s - pltpu.repeat(m_new, n_rep_s, axis=1))
        l_ref[c] = scale * l_ref[c] + jnp.sum(p, axis=1, keepdims=True)
        acc_ref[c] = pltpu.repeat(scale, n_rep_v, axis=1) * acc_ref[c] + _dot(p.astype(BF16), vc)
        m_ref[c] = m_new

    kc, vc = kv_chunk(qi)
    for c in range(2):
        s = scores(c, kc) + bdiag_ref[...]
        m = jnp.broadcast_to(jnp.max(s, axis=1, keepdims=True), (tq, V7X_LANES))
        p = jnp.exp2(s - pltpu.repeat(m, n_rep_s, axis=1))
        m_ref[c] = m
        l_ref[c] = jnp.broadcast_to(jnp.sum(p, axis=1, keepdims=True), (tq, V7X_LANES))
        acc_ref[c] = _dot(p.astype(BF16), vc)

    @pl.when(qi >= 1)
    def _sub_diagonal():
        kc1, vc1 = kv_chunk(qi - 1)
        for c in range(2):
            update(c, scores(c, kc1) + bsub_ref[...], vc1)

    def far_body(j, carry):
        kcj, vcj = kv_chunk(j)
        for c in range(2):
            update(c, scores(c, kcj), vcj)
        return carry

    lax.fori_loop(0, jnp.maximum(qi - 1, 0), far_body, 0)

    lam = _lambda_value(lq1_ref, lk1_ref, lq2_ref, lk2_ref, lam_init)
    w1 = pltpu.repeat(1.0 / l_ref[0], n_rep_v, axis=1)
    w2 = pltpu.repeat(lam / l_ref[1], n_rep_v, axis=1)
    o = acc_ref[0] * w1 - acc_ref[1] * w2
    o = o * lax.rsqrt(jnp.mean(o * o, axis=1, keepdims=True) + RMS_EPS) * subln_ref[...] * (1.0 - lam_init)
    o_ref[...] = o.astype(o_ref.dtype)


def _diff_attention_prompt(qd, kd, vd, lam_vecs, vtab, subln_w, *, lam_init, tq=512):
    b, t, w = qd.shape
    h = w // DIFF_V_DIM
    tq = _pick_tile(t, tq)
    vec = pl.BlockSpec((1, DIFF_HEAD_DIM), lambda bi, hi, qi: (0, 0))
    in_specs = [vec, vec, vec, vec,
                pl.BlockSpec((None, 1, V7X_LANES), lambda bi, hi, qi: (hi, 0, 0)),
                pl.BlockSpec((1, DIFF_V_DIM), lambda bi, hi, qi: (0, 0)),
                pl.BlockSpec((None, tq, DIFF_V_DIM), lambda bi, hi, qi: (bi, qi, hi)),
                pl.BlockSpec((None, t, DIFF_V_DIM), lambda bi, hi, qi: (bi, 0, hi)),
                pl.BlockSpec((None, t, DIFF_V_DIM), lambda bi, hi, qi: (bi, 0, hi))]
    kern = functools.partial(_diff_attn_kernel, tq=tq, lam_init=lam_init)
    return pl.pallas_call(
        kern, grid=(b, h, t // tq), in_specs=in_specs,
        out_specs=pl.BlockSpec((None, tq, DIFF_V_DIM), lambda bi, hi, qi: (bi, qi, hi)),
        out_shape=jax.ShapeDtypeStruct((b, t, w), BF16),
        scratch_shapes=[pltpu.VMEM((tq, tq), F32), pltpu.VMEM((tq, tq), F32),
                        pltpu.VMEM((2, tq, V7X_LANES), F32), pltpu.VMEM((2, tq, V7X_LANES), F32),
                        pltpu.VMEM((2, tq, DIFF_V_DIM), F32)],
        compiler_params=_cparams(("arbitrary", "arbitrary", "arbitrary")), name="diff_attn_prompt",
    )(*lam_vecs, vtab.reshape(h, 1, V7X_LANES), subln_w.reshape(1, DIFF_V_DIM).astype(F32), qd, kd, vd)


def _later_matrix(n):
    u = (np.arange(n)[:, None] > np.arange(n)[None, :]).astype(np.float32)
    return jnp.asarray(np.concatenate([u, u], axis=0), dtype=BF16)


def _stick_break_terms(z):
    sp = jnp.log2(1.0 + jnp.exp2(-jnp.abs(z)))
    return jnp.minimum(z, 0.0) - sp, jnp.minimum(-z, 0.0) - sp


def _split_bf16(x):
    hi = x.astype(BF16)
    lo = (x - hi.astype(F32)).astype(BF16)
    return hi, lo


def _sb_attn_kernel(u2_ref, q_ref, k_ref, v_ref, o_ref, c_ref, acc_ref, *, tq, tk):
    qi = pl.program_id(2)
    n_sub = tq // tk
    q = q_ref[...]
    c_ref[...] = jnp.zeros_like(c_ref)
    acc_ref[...] = jnp.zeros_like(acc_ref)

    def process(chunks):
        c = c_ref[...]
        weights, values = [], []
        for start, diag_offset in chunks:
            kc = k_ref[pl.ds(start, tk), :]
            values.append(v_ref[pl.ds(start, tk), :])
            log_beta, log_rest = _stick_break_terms(_nt_dot(q, kc))
            if diag_offset is not None:
                ii = lax.broadcasted_iota(jnp.int32, (tq, tk), 0)
                jj = lax.broadcasted_iota(jnp.int32, (tq, tk), 1)
                mask = (jj + diag_offset) < ii
                log_rest = jnp.where(mask, log_rest, 0.0)
            hi, lo = _split_bf16(log_rest)
            later = _dot(jnp.concatenate([hi, lo], axis=1), u2_ref[...]) + pltpu.repeat(c, tk // V7X_LANES, axis=1)
            a = jnp.exp2(log_beta + later)
            if diag_offset is not None:
                a = jnp.where(mask, a, 0.0)
            weights.append(a.astype(BF16))
            c = c + jnp.sum(log_rest, axis=1, keepdims=True)
        c_ref[...] = c
        acc_ref[...] += _dot(jnp.concatenate(weights, axis=1), jnp.concatenate(values, axis=0))

    process([(pl.multiple_of(qi * tq + r * tk, tk), r * tk) for r in reversed(range(n_sub))])

    n_far = qi

    def far_body(j, carry):
        base = (n_far - 1 - j) * tq
        process([(pl.multiple_of(base + r * tk, tk), None) for r in reversed(range(n_sub))])
        return carry

    lax.fori_loop(0, n_far, far_body, 0)
    o_ref[...] = acc_ref[...].astype(o_ref.dtype)


def _sb_attention_prompt(qs, ks, vs, *, tq=512, tk=256):
    b, t, w = qs.shape
    h = w // SB_HEAD_DIM
    tq = _pick_tile(t, tq)
    tk = _pick_tile(tq, tk)
    in_specs = [pl.BlockSpec((2 * tk, tk), lambda bi, hi, qi: (0, 0)),
                pl.BlockSpec((None, tq, SB_HEAD_DIM), lambda bi, hi, qi: (bi, qi, hi)),
                pl.BlockSpec((None, t, SB_HEAD_DIM), lambda bi, hi, qi: (bi, 0, hi)),
                pl.BlockSpec((None, t, SB_HEAD_DIM), lambda bi, hi, qi: (bi, 0, hi))]
    kern = functools.partial(_sb_attn_kernel, tq=tq, tk=tk)
    return pl.pallas_call(
        kern, grid=(b, h, t // tq), in_specs=in_specs,
        out_specs=pl.BlockSpec((None, tq, SB_HEAD_DIM), lambda bi, hi, qi: (bi, qi, hi)),
        out_shape=jax.ShapeDtypeStruct((b, t, w), BF16),
        scratch_shapes=[pltpu.VMEM((tq, V7X_LANES), F32), pltpu.VMEM((tq, SB_HEAD_DIM), F32)],
        compiler_params=_cparams(("parallel", "parallel", "parallel")), name="sb_attn_prompt",
    )(_later_matrix(tk), qs, ks, vs)


def _expand_queries(q, n_groups, group_cols):
    r, c = q.shape
    tiled = jnp.tile(q, (n_groups, 1))
    row_g = lax.broadcasted_iota(jnp.int32, (n_groups * r, c), 0) >> _log2(r)
    col_g = lax.broadcasted_iota(jnp.int32, (n_groups * r, c), 1) >> _log2(group_cols)
    return jnp.where(row_g == col_g, tiled, 0.0).astype(BF16)


def _log2(n):
    assert n & (n - 1) == 0
    return n.bit_length() - 1


def _sample_attn_kernel(pt_ref, lq1_ref, lk1_ref, lq2_ref, lk2_ref, subln_ref, btail_ref, ut_ref,
                        ed_ref, md_ref, es_ref, ms_ref,
                        qd_ref, qs_ref, knd_ref, kns_ref, vnd_ref, vns_ref, *rest,
                        pages_per_step, n_pages, n_q, n_new, lam_init):
    del pt_ref
    pps = pages_per_step
    kd_refs, ks_refs = rest[0:pps], rest[pps:2 * pps]
    vd_refs, vs_refs = rest[2 * pps:3 * pps], rest[3 * pps:4 * pps]
    od_ref, os_ref, qexp_ref, s_ref, a_ref, accd_ref, accs_ref = rest[4 * pps:]
    step = pl.program_id(1)
    n_steps = n_pages // pps
    page = V7X_LANES
    n_past = n_pages * page
    lanes = V7X_LANES
    pair = 2 * n_q
    n_pair = lanes // pair
    n_groups = lanes // n_q

    def pad_to_page(x):
        return jnp.concatenate([x, jnp.zeros((page - n_new, x.shape[1]), x.dtype)], axis=0)

    def keys_by_group(k_ref):
        slabs = [k_ref[pl.ds(g, page, stride=n_groups), :] for g in range(n_groups)]
        return jnp.concatenate(slabs, axis=1).astype(BF16)

    @pl.when(step == 0)
    def _new_token_scores():
        qexp_ref[0] = _expand_queries(qd_ref[...], n_groups, DIFF_HEAD_DIM)
        qexp_ref[1] = _expand_queries(qs_ref[...], n_groups, SB_HEAD_DIM)
        s_ref[0, n_past:, :] = _nt_dot(knd_ref[...], qexp_ref[0])
        s_ref[1, n_past:, :] = _nt_dot(kns_ref[...], qexp_ref[1])

    @pl.when(step < n_steps)
    def _key_phase():
        for p in range(pps):
            row = pl.multiple_of((step * pps + p) * page, page)
            s_ref[0, pl.ds(row, page), :] = _nt_dot(keys_by_group(kd_refs[p]), qexp_ref[0])
            s_ref[1, pl.ds(row, page), :] = _nt_dot(keys_by_group(ks_refs[p]), qexp_ref[1])

    @pl.when(step == n_steps)
    def _attention_weights():
        lane = lax.broadcasted_iota(jnp.int32, (1, lanes), 1)
        lam = _lambda_value(lq1_ref, lk1_ref, lq2_ref, lk2_ref, lam_init)
        s_ref[0, n_past - page:, :] = s_ref[0, n_past - page:, :] + btail_ref[...]
        s = s_ref[0]
        m = jnp.max(s, axis=0, keepdims=True)
        p = jnp.exp2(s - m)
        is_first = ((lane >> _log2(n_q)) & 1) == 0
        coef = jnp.where(is_first, 1.0, -lam) / jnp.sum(p, axis=0, keepdims=True)
        for pg in range(n_pages):
            a_ref[0, pg] = jnp.transpose(p[pg * page:(pg + 1) * page] * coef).astype(BF16)
        a_ref[0, n_pages] = jnp.transpose(pad_to_page(p[n_past:] * coef)).astype(BF16)

        log_beta, log_rest = _stick_break_terms(s_ref[1])
        row = lax.broadcasted_iota(jnp.int32, (page, lanes), 0)
        new_ok = row < (lax.broadcasted_iota(jnp.int32, (page, lanes), 1) & (n_q - 1))
        ut = ut_ref[...]
        rest_new = jnp.where(new_ok, pad_to_page(log_rest[n_past:]), 0.0)
        hi, lo = _split_bf16(rest_new)
        later = _dot(ut, hi) + _dot(ut, lo)
        a_new = jnp.where(new_ok, jnp.exp2(pad_to_page(log_beta[n_past:]) + later), 0.0)
        a_ref[1, n_pages] = jnp.transpose(a_new).astype(BF16)
        carry = jnp.sum(rest_new, axis=0, keepdims=True)
        for pg in reversed(range(n_pages)):
            rows = slice(pg * page, (pg + 1) * page)
            hi, lo = _split_bf16(log_rest[rows])
            later = _dot(ut, hi) + _dot(ut, lo) + carry
            a_ref[1, pg] = jnp.transpose(jnp.exp2(log_beta[rows] + later)).astype(BF16)
            carry = carry + jnp.sum(log_rest[rows], axis=0, keepdims=True)

        a_new_d, a_new_s = a_ref[0, n_pages], a_ref[1, n_pages]
        vn_d, vn_s = pad_to_page(vnd_ref[...]), pad_to_page(vns_ref[...])
        for hp in range(n_pair):
            rows = slice(hp * pair, (hp + 1) * pair)
            cols = slice(hp * DIFF_V_DIM, (hp + 1) * DIFF_V_DIM)
            accd_ref[rows, :] = _dot(a_new_d[rows, :], vn_d[:, cols])
            both = _dot(a_new_s[rows, :], vn_s[:, cols])
            accs_ref[hp * pair:hp * pair + n_q, :] = both[:n_q, :SB_HEAD_DIM]
            accs_ref[hp * pair + n_q:(hp + 1) * pair, :] = both[n_q:, SB_HEAD_DIM:]

    @pl.when(step >= n_steps)
    def _value_phase():
        for p in range(pps):
            pg = (step - n_steps) * pps + p
            for g, e_ref, m_ref, v_refs, acc in ((0, ed_ref, md_ref, vd_refs, accd_ref),
                                                 (1, es_ref, ms_ref, vs_refs, accs_ref)):
                spread = _dot(a_ref[g, pg], e_ref[...]).astype(BF16) * m_ref[...]
                acc[...] += _dot(spread, v_refs[p][...].astype(BF16))

    @pl.when(step == 2 * n_steps - 1)
    def _finish():
        for hp in range(n_pair):
            first = accd_ref[hp * pair:hp * pair + n_q, :]
            second = accd_ref[hp * pair + n_q:(hp + 1) * pair, :]
            o = first + second
            o = (o * lax.rsqrt(jnp.mean(o * o, axis=1, keepdims=True) + RMS_EPS)
                 * subln_ref[...] * (1.0 - lam_init))
            od_ref[:, hp * DIFF_V_DIM:(hp + 1) * DIFF_V_DIM] = o.astype(od_ref.dtype)
        for h in range(n_groups):
            os_ref[:, h * SB_HEAD_DIM:(h + 1) * SB_HEAD_DIM] = accs_ref[h * n_q:(h + 1) * n_q, :].astype(os_ref.dtype)


def _sample_tail_bias(rel_bias, n_q, n_new, past_len):
    del past_len
    page = V7X_LANES
    n_rows = page + n_new
    near = _near_bias_log2(rel_bias)
    n_heads = near.shape[0]
    by_dist = jnp.concatenate([jnp.full((n_heads, n_new), NEG_INF, F32), near,
                               jnp.zeros((n_heads, n_q), F32)], axis=1)
    falling = by_dist[:, ::-1]
    per_query = [falling[:, n_q - 1 - i:n_q - 1 - i + n_rows] for i in range(n_q)]
    tail = jnp.stack(per_query, axis=-1)
    tail = jnp.broadcast_to(tail[:, :, None, :], (n_heads, n_rows, 2, n_q))
    return jnp.transpose(tail, (1, 0, 2, 3)).reshape(n_rows, n_heads * 2 * n_q)


def _sample_attention(qd, qs, knd, kns, vnd, vns, cache_kd, cache_ks, cache_vd, cache_vs, page_table,
                      lam_vecs, subln_w, rel_bias, *, lam_init, pages_per_step=4):
    bsz, n_q, w = qd.shape
    n_pages = page_table.shape[1]
    page = V7X_LANES
    n_groups = V7X_LANES // n_q
    n_heads_d = w // DIFF_V_DIM
    assert w == 8 * DIFF_V_DIM and n_q == 8
    assert cache_kd.shape[1:] == (page * n_groups, DIFF_HEAD_DIM) == cache_ks.shape[1:] == cache_vs.shape[1:]
    assert cache_vd.shape[1:] == (page * n_heads_d, DIFF_V_DIM)
    n_new = 2 * n_q
    pps = pages_per_step
    n_steps = n_pages // pps
    past_len = n_pages * page

    def pad_new(x):
        return jnp.pad(x, ((0, 0), (0, n_new - n_q), (0, 0)))

    def fixed(shape):
        return pl.BlockSpec(shape, lambda b, s, pt: (0,) * len(shape))

    def per_seq(rows):
        return pl.BlockSpec((None, rows, w), lambda b, s, pt: (b, 0, 0))

    def key_page(p, shape):
        return pl.BlockSpec((None,) + shape,
                            lambda b, s, pt: (pt[b, jnp.minimum(s, n_steps - 1) * pps + p], 0, 0))

    def value_page(p, shape):
        return pl.BlockSpec((None,) + shape,
                            lambda b, s, pt: (pt[b, jnp.maximum(s - n_steps, 0) * pps + p], 0, 0))

    def spread_and_keep(n_heads, rows_per_head):
        col = np.arange(page * n_heads)
        spread = (np.arange(page)[:, None] == col[None, :] // n_heads).astype(np.float32)
        keep = (np.arange(V7X_LANES)[:, None] // rows_per_head == col[None, :] % n_heads).astype(np.float32)
        return jnp.asarray(spread, dtype=BF16), jnp.asarray(keep, dtype=BF16)

    e_d, m_d = spread_and_keep(n_heads_d, 2 * n_q)
    e_s, m_s = spread_and_keep(n_groups, n_q)
    k_shape, vd_shape = (page * n_groups, DIFF_HEAD_DIM), (page * n_heads_d, DIFF_V_DIM)
    vec = fixed((1, DIFF_HEAD_DIM))
    in_specs = ([vec, vec, vec, vec, fixed((1, DIFF_V_DIM)), fixed((page + n_new, V7X_LANES)),
                 fixed((page, page)), fixed(e_d.shape), fixed(m_d.shape), fixed(e_s.shape), fixed(m_s.shape),
                 per_seq(n_q), per_seq(n_q),
                 per_seq(n_new), per_seq(n_new), per_seq(n_new), per_seq(n_new)]
                + [key_page(p, k_shape) for p in range(pps)] * 2
                + [value_page(p, vd_shape) for p in range(pps)] + [value_page(p, k_shape) for p in range(pps)])
    ut = jnp.asarray((np.arange(page)[None, :] > np.arange(page)[:, None]).astype(np.float32), dtype=BF16)
    args = ([*lam_vecs, subln_w.reshape(1, DIFF_V_DIM).astype(F32),
             _sample_tail_bias(rel_bias, n_q, n_new, past_len), ut, e_d, m_d, e_s, m_s,
             qd, qs, pad_new(knd), pad_new(kns), pad_new(vnd), pad_new(vns)]
            + [cache_kd] * pps + [cache_ks] * pps + [cache_vd] * pps + [cache_vs] * pps)
    kern = functools.partial(_sample_attn_kernel, pages_per_step=pps, n_pages=n_pages, n_q=n_q,
                             n_new=n_new, lam_init=lam_init)
    n_keys = past_len + n_new
    grid_spec = pltpu.PrefetchScalarGridSpec(
        num_scalar_prefetch=1, grid=(bsz, 2 * n_steps), in_specs=in_specs,
        out_specs=[pl.BlockSpec((None, n_q, w), lambda b, s, pt: (b, 0, 0))] * 2,
        scratch_shapes=[pltpu.VMEM((2, V7X_LANES, w), BF16),
                        pltpu.VMEM((2, n_keys, V7X_LANES), F32),
                        pltpu.VMEM((2, n_pages + 1, V7X_LANES, page), BF16),
                        pltpu.VMEM((V7X_LANES, DIFF_V_DIM), F32),
                        pltpu.VMEM((V7X_LANES, SB_HEAD_DIM), F32)])
    return pl.pallas_call(
        kern, grid_spec=grid_spec,
        out_shape=[jax.ShapeDtypeStruct((bsz, n_q, w), F32)] * 2,
        compiler_params=_cparams(("arbitrary", "arbitrary")), name="sample_attn",
    )(page_table, *args)


def _cross_attn_prompt_kernel(q_ref, k_ref, v_ref, o_ref):
    s = _nt_dot(q_ref[...], k_ref[...])
    p = jnp.exp2(s - jnp.max(s, axis=1, keepdims=True))
    o = _dot(p.astype(BF16), v_ref[...]) / jnp.sum(p, axis=1, keepdims=True)
    o_ref[...] = o.astype(o_ref.dtype)


def _cross_attention_prompt(q, mk, mv, *, tq=1024):
    b, t, d = q.shape
    mem_len = mk.shape[1]
    hd = d // MEM_HEADS
    tq = _pick_tile(t, tq)
    q_spec = pl.BlockSpec((None, tq, hd), lambda bi, qi, hi: (bi, qi, hi))
    m_spec = pl.BlockSpec((None, mem_len, hd), lambda bi, qi, hi: (bi, 0, hi))
    return pl.pallas_call(
        _cross_attn_prompt_kernel, grid=(b, t // tq, MEM_HEADS), in_specs=[q_spec, m_spec, m_spec],
        out_specs=q_spec, out_shape=jax.ShapeDtypeStruct((b, t, d), BF16),
        compiler_params=_cparams(("parallel", "parallel", "parallel")), name="cross_attn_prompt",
    )(q, mk, mv)


def _cross_attn_sample_kernel(q_ref, k_ref, v_ref, o_ref, *, n_q):
    hd = k_ref.shape[2]
    qexp = _expand_queries(q_ref[...], MEM_HEADS, hd)
    k = jnp.concatenate([k_ref[:, h, :] for h in range(MEM_HEADS)], axis=1)
    s = _nt_dot(k.astype(BF16), qexp)
    p = jnp.exp2(s - jnp.max(s, axis=0, keepdims=True))
    p = jnp.transpose(p / jnp.sum(p, axis=0, keepdims=True))
    for h in range(MEM_HEADS):
        o = _dot(p[h * n_q:(h + 1) * n_q].astype(BF16), v_ref[:, h, :].astype(BF16))
        o_ref[:, h * hd:(h + 1) * hd] = o.astype(o_ref.dtype)


def _cross_attention_sample(q, mem_k, mem_v):
    b, n_q, d = q.shape
    mem_len = mem_k.shape[1]
    q_spec = pl.BlockSpec((None, n_q, d), lambda bi: (bi, 0, 0))
    m_spec = pl.BlockSpec((None, mem_len, MEM_HEADS, d // MEM_HEADS), lambda bi: (bi, 0, 0, 0))
    return pl.pallas_call(
        functools.partial(_cross_attn_sample_kernel, n_q=n_q), grid=(b,),
        in_specs=[q_spec, m_spec, m_spec], out_specs=q_spec,
        out_shape=jax.ShapeDtypeStruct((b, n_q, d), F32),
        compiler_params=_cparams(("parallel",)), name="cross_attn_sample",
    )(q, mem_k, mem_v)


def _post_attention(x, mixed, cross_fn, wts, alpha, q_dtype):
    (w_out, ln1_g, ln1_b, w_cq, w_co, ln2_g, ln2_b, w_up, b_up, w_down, b_down, ln3_g, ln3_b) = wts
    d = x.shape[1]
    (y,) = _matmul(mixed, w_out, out_dtypes=[F32], res=x, alpha=alpha, name="mm_out")
    x1, x1b = _layer_norm(y, ln1_g, ln1_b, name="ln1")
    (qc,) = _matmul(x1b, w_cq, out_dtypes=[q_dtype], out_scale=(d // MEM_HEADS) ** -0.5 * LOG2E, name="mm_cq")
    oc = cross_fn(qc).astype(BF16)
    (y,) = _matmul(oc, w_co, out_dtypes=[F32], res=x1, alpha=alpha, name="mm_co")
    x2, x2b = _layer_norm(y, ln2_g, ln2_b, name="ln2")
    (hid,) = _matmul(x2b, w_up, out_dtypes=[BF16], bias=b_up, act="relu2", name="mm_up")
    (y,) = _matmul(hid, w_down, out_dtypes=[F32], bias=b_down, res=x2, alpha=alpha, tk=2048, name="mm_down")
    x3, _ = _layer_norm(y, ln3_g, ln3_b, name="ln3")
    return x3


def _project_qkv(xb, w_in_b, q_scale, q_dtype):
    dw = w_in_b.shape[1] // 6
    sec = [(i * dw, dw) for i in range(6)]
    (qd,) = _matmul(xb, w_in_b, cols=sec[0], out_dtypes=[q_dtype], out_scale=q_scale, name="mm_qd")
    kd32, kd = _matmul(xb, w_in_b, cols=sec[1], out_dtypes=[F32, BF16], name="mm_kd")
    vd32, vd = _matmul(xb, w_in_b, cols=sec[2], out_dtypes=[F32, BF16], name="mm_vd")
    (qs,) = _matmul(xb, w_in_b, cols=sec[3], out_dtypes=[q_dtype], out_scale=q_scale, name="mm_qs")
    ks32, ks = _matmul(xb, w_in_b, cols=sec[4], out_dtypes=[F32, BF16], name="mm_ks")
    vs32, vs = _matmul(xb, w_in_b, cols=sec[5], out_dtypes=[F32, BF16], name="mm_vs")
    return (qd, kd, vd, qs, ks, vs), (kd32, vd32, ks32, vs32)


def kernel(x_prompt, x_sample, mem_prompt, cache_diff_k, cache_diff_v, cache_sb_k, cache_sb_v, cache_mem_k, cache_mem_v, page_table, w_in, lambda_q1, lambda_k1, lambda_q2, lambda_k2, subln_w, rel_bias, w_out, ln1_g, ln1_b, w_cq, w_ck, w_cv, w_co, ln2_g, ln2_b, w_up, b_up, w_down, b_down, ln3_g, ln3_b):
    depth = w_in.shape[0]
    assert depth == 1, "single-layer step"
    bsz, seq, d = x_prompt.shape
    dec_b, dec_t, _ = x_sample.shape
    mem_len = mem_prompt.shape[1]
    n_pool, page = cache_diff_k.shape[1], cache_diff_k.shape[2]
    dw = d // 2
    alpha = (2.0 * depth) ** 0.25
    l = 0
    lam_init = 0.8 - 0.6 * math.exp(-0.3 * l)
    q_scale = DIFF_HEAD_DIM ** -0.5 * LOG2E
    assert DIFF_HEAD_DIM == SB_HEAD_DIM

    w_in_b = w_in[l].astype(BF16)
    tail_w = (w_out[l].astype(BF16), ln1_g[l], ln1_b[l], w_cq[l].astype(BF16), w_co[l].astype(BF16),
              ln2_g[l], ln2_b[l], w_up[l].astype(BF16), b_up[l], w_down[l].astype(BF16), b_down[l],
              ln3_g[l], ln3_b[l])
    lam_vecs = [v[l].reshape(1, DIFF_HEAD_DIM).astype(F32) for v in (lambda_q1, lambda_k1, lambda_q2, lambda_k2)]
    vnear = _near_bias_log2(rel_bias)
    vtab = jnp.concatenate([vnear[:, :1], vnear[:, :0:-1]], axis=1)

    xp = x_prompt.reshape(bsz * seq, d)
    (qd, kd, vd, qs, ks, vs), (kd32, vd32, ks32, vs32) = _project_qkv(xp.astype(BF16), w_in_b, q_scale,
                                                                      BF16)
    shape3 = (bsz, seq, dw)
    mixed_d = _diff_attention_prompt(qd.reshape(shape3), kd.reshape(shape3), vd.reshape(shape3), lam_vecs,
                                     vtab, subln_w[l], lam_init=lam_init)
    mixed_s = _sb_attention_prompt(qs.reshape(shape3), ks.reshape(shape3), vs.reshape(shape3))
    mixed = jnp.concatenate([mixed_d, mixed_s], axis=-1).reshape(bsz * seq, d)
    memb = mem_prompt.reshape(bsz * mem_len, d).astype(BF16)
    mk32, mk = _matmul(memb, w_ck[l].astype(BF16), out_dtypes=[F32, BF16], name="mm_ck")
    mv32, mv = _matmul(memb, w_cv[l].astype(BF16), out_dtypes=[F32, BF16], name="mm_cv")

    def cross_prompt(qc):
        o = _cross_attention_prompt(qc.reshape(bsz, seq, d), mk.reshape(bsz, mem_len, d),
                                    mv.reshape(bsz, mem_len, d))
        return o.reshape(bsz * seq, d)

    y_prompt = _post_attention(xp, mixed, cross_prompt, tail_w, alpha, BF16).reshape(bsz, seq, d)

    xs = x_sample.reshape(dec_b * dec_t, d)
    (sqd, skd, svd, sqs, sks, svs), (skd32, svd32, sks32, svs32) = _project_qkv(xs.astype(BF16), w_in_b,
                                                                              q_scale, F32)
    s3 = (dec_b, dec_t, dw)
    smix_d, smix_s = _sample_attention(
        sqd.reshape(s3), sqs.reshape(s3), skd.reshape(s3), sks.reshape(s3), svd.reshape(s3), svs.reshape(s3),
        cache_diff_k[l].reshape(n_pool, -1, DIFF_HEAD_DIM), cache_sb_k[l].reshape(n_pool, -1, SB_HEAD_DIM),
        cache_diff_v[l].reshape(n_pool, -1, DIFF_V_DIM), cache_sb_v[l].reshape(n_pool, -1, SB_HEAD_DIM),
        page_table, lam_vecs, subln_w[l], rel_bias, lam_init=lam_init)
    smixed = jnp.concatenate([smix_d, smix_s], axis=-1).reshape(dec_b * dec_t, d).astype(BF16)

    def cross_sample(qc):
        o = _cross_attention_sample(qc.reshape(dec_b, dec_t, d), cache_mem_k[l], cache_mem_v[l])
        return o.reshape(dec_b * dec_t, d)

    y_sample = _post_attention(xs, smixed, cross_sample, tail_w, alpha, F32).reshape(dec_b, dec_t, d)

    dh, hd = DIFF_HEAD_DIM, dw // DIFF_V_DIM
    hs = dw // SB_HEAD_DIM
    mh = d // MEM_HEADS
    return (y_prompt, y_sample,
            kd32.reshape(1, bsz, seq, hd, 2, dh), vd32.reshape(1, bsz, seq, hd, 2 * dh),
            ks32.reshape(1, bsz, seq, hs, SB_HEAD_DIM), vs32.reshape(1, bsz, seq, hs, SB_HEAD_DIM),
            mk32.reshape(1, bsz, mem_len, MEM_HEADS, mh), mv32.reshape(1, bsz, mem_len, MEM_HEADS, mh),
            skd32.reshape(1, dec_b, dec_t, hd, 2, dh), svd32.reshape(1, dec_b, dec_t, hd, 2 * dh),
            sks32.reshape(1, dec_b, dec_t, hs, SB_HEAD_DIM), svs32.reshape(1, dec_b, dec_t, hs, SB_HEAD_DIM))
```

```python
import functools
import math

import numpy as np
import jax
import jax.numpy as jnp
from jax import lax
from jax.experimental import pallas as pl
from jax.experimental.pallas import tpu as pltpu

F32 = jnp.float32
BF16 = jnp.bfloat16

DIFF_HEAD_DIM = 128
DIFF_V_DIM = 2 * DIFF_HEAD_DIM
SB_HEAD_DIM = 128
MEM_HEADS = 4
N_BUCKETS = 32
MAX_DISTANCE = 128
LN_EPS = 1e-5
RMS_EPS = 1e-5
NEG_INF = -1e30
LOG2E = math.log2(math.e)

V7X_LANES = 128
V7X_VMEM_BYTES = 64 * 1024 * 1024
VMEM_LIMIT_BYTES = V7X_VMEM_BYTES - 8 * 1024 * 1024


def _cparams(semantics):
    return pltpu.CompilerParams(dimension_semantics=semantics, vmem_limit_bytes=VMEM_LIMIT_BYTES)


def _nt_dot(a, b):
    return lax.dot_general(a, b, (((1,), (1,)), ((), ())), preferred_element_type=F32)


def _dot(a, b):
    return jnp.dot(a, b, preferred_element_type=F32)


def _tile_lanes(x, n):
    return x if n == 1 else jnp.concatenate([x] * n, axis=1)


def _mm_kernel(*refs, nk, has_bias, has_res, act, alpha, out_scale, n_out):
    x_ref, w_ref = refs[0], refs[1]
    pos = 2
    bias_ref = res_ref = None
    if has_bias:
        bias_ref = refs[pos]
        pos += 1
    if has_res:
        res_ref = refs[pos]
        pos += 1
    out_refs = refs[pos:pos + n_out]
    acc_ref = refs[pos + n_out] if nk > 1 else None

    def epilogue(y):
        if has_bias:
            y = y + bias_ref[...]
        if act == "relu2":
            y = jnp.square(jnp.maximum(y, 0.0))
        if has_res:
            y = alpha * res_ref[...] + y
        if out_scale != 1.0:
            y = y * out_scale
        for o_ref in out_refs:
            o_ref[...] = y.astype(o_ref.dtype)

    part = _dot(x_ref[...], w_ref[...])
    if nk == 1:
        epilogue(part)
    else:
        k = pl.program_id(2)

        @pl.when(k == 0)
        def _():
            acc_ref[...] = part

        @pl.when(jnp.logical_and(k > 0, k < nk - 1))
        def _():
            acc_ref[...] += part

        @pl.when(k == nk - 1)
        def _():
            epilogue(acc_ref[...] + part)


def _pick_tile(n, pref):
    t = min(n, pref)
    while n % t:
        t //= 2
    return t


def _matmul(x, w, *, out_dtypes, bias=None, res=None, act=None, alpha=1.0, out_scale=1.0,
            tm=1024, tn=1024, tk=4096, cols=None, name="mm"):
    m, kdim = x.shape
    col0, n = cols if cols is not None else (0, w.shape[1])
    tm, tn, tk = _pick_tile(m, tm), _pick_tile(n, tn), _pick_tile(kdim, tk)
    assert col0 % tn == 0
    j0 = col0 // tn
    nk = kdim // tk
    in_specs = [pl.BlockSpec((tm, tk), lambda i, j, k: (i, k)),
                pl.BlockSpec((tk, tn), lambda i, j, k: (k, j + j0))]
    args = [x, w]
    if bias is not None:
        in_specs.append(pl.BlockSpec((1, tn), lambda i, j, k: (0, j)))
        args.append(bias.reshape(1, n).astype(F32))
    if res is not None:
        in_specs.append(pl.BlockSpec((tm, tn), lambda i, j, k: (i, j)))
        args.append(res)
    out_specs = [pl.BlockSpec((tm, tn), lambda i, j, k: (i, j)) for _ in out_dtypes]
    out_shape = [jax.ShapeDtypeStruct((m, n), dt) for dt in out_dtypes]
    scratch = [pltpu.VMEM((tm, tn), F32)] if nk > 1 else []
    kern = functools.partial(_mm_kernel, nk=nk, has_bias=bias is not None, has_res=res is not None,
                             act=act, alpha=alpha, out_scale=out_scale, n_out=len(out_dtypes))
    return pl.pallas_call(
        kern, grid=(m // tm, n // tn, nk), in_specs=in_specs, out_specs=out_specs,
        out_shape=out_shape, scratch_shapes=scratch,
        compiler_params=_cparams(("parallel", "parallel", "arbitrary")), name=name,
    )(*args)


def _ln_kernel(y_ref, g_ref, b_ref, o32_ref, o16_ref):
    y = y_ref[...]
    mu = jnp.mean(y, axis=-1, keepdims=True)
    d = y - mu
    var = jnp.mean(jnp.square(d), axis=-1, keepdims=True)
    o = d * lax.rsqrt(var + LN_EPS) * g_ref[...] + b_ref[...]
    o32_ref[...] = o
    o16_ref[...] = o.astype(BF16)


def _layer_norm(y, g, b, *, tm=256, name="ln"):
    m, d = y.shape
    tm = _pick_tile(m, tm)
    row = pl.BlockSpec((tm, d), lambda i: (i, 0))
    vec = pl.BlockSpec((1, d), lambda i: (0, 0))
    return pl.pallas_call(
        _ln_kernel, grid=(m // tm,), in_specs=[row, vec, vec], out_specs=[row, row],
        out_shape=[jax.ShapeDtypeStruct((m, d), F32), jax.ShapeDtypeStruct((m, d), BF16)],
        compiler_params=_cparams(("parallel",)), name=name,
    )(y, g.reshape(1, d).astype(F32), b.reshape(1, d).astype(F32))


def _bucket_table(n):
    max_exact = N_BUCKETS // 2
    d = np.arange(n)
    df = np.maximum(d, 1).astype(np.float32)
    ratio = np.log(df / np.float32(max_exact)) / np.float32(math.log(MAX_DISTANCE / max_exact))
    large = max_exact + (ratio * np.float32(N_BUCKETS - max_exact)).astype(np.int32)
    large = np.minimum(large, N_BUCKETS - 1)
    return np.where(d < max_exact, d, large).astype(np.int32)


def _near_bias_log2(rel_bias):
    tab = rel_bias.astype(F32)[_bucket_table(MAX_DISTANCE)]
    far = rel_bias.astype(F32)[N_BUCKETS - 1]
    return jnp.transpose(tab - far[None, :]) * LOG2E


def _lambda_value(lq1_ref, lk1_ref, lq2_ref, lk2_ref, lam_init):
    s1 = jnp.sum(lq1_ref[...] * lk1_ref[...], axis=1, keepdims=True)
    s2 = jnp.sum(lq2_ref[...] * lk2_ref[...], axis=1, keepdims=True)
    return jnp.exp(s1) - jnp.exp(s2) + lam_init


def _diff_attn_kernel(lq1_ref, lk1_ref, lq2_ref, lk2_ref, vtab_ref, subln_ref, q_ref, k_ref, v_ref,
                      o_ref, bdiag_ref, bsub_ref, m_ref, l_ref, acc_ref, *, tq, lam_init):
    qi = pl.program_id(2)
    dh = DIFF_HEAD_DIM
    nb = tq // V7X_LANES

    @pl.when(qi == 0)
    def _build_bias_tiles():
        toep = pltpu.roll(jnp.broadcast_to(vtab_ref[...], (V7X_LANES, V7X_LANES)), 0, 1,
                          stride=1, stride_axis=0)
        ii = lax.broadcasted_iota(jnp.int32, (V7X_LANES, V7X_LANES), 0)
        jj = lax.broadcasted_iota(jnp.int32, (V7X_LANES, V7X_LANES), 1)
        on_diag = jnp.where(ii >= jj, toep, NEG_INF)
        below = jnp.where(jj > ii, toep, 0.0)
        zeros = jnp.zeros((V7X_LANES, V7X_LANES), F32)
        masked = jnp.full((V7X_LANES, V7X_LANES), NEG_INF, F32)
        for rb in range(nb):
            for cb in range(nb):
                rows = slice(rb * V7X_LANES, (rb + 1) * V7X_LANES)
                cols = slice(cb * V7X_LANES, (cb + 1) * V7X_LANES)
                if rb == cb:
                    tile = on_diag
                elif rb == cb + 1:
                    tile = below
                elif rb > cb:
                    tile = zeros
                else:
                    tile = masked
                bdiag_ref[rows, cols] = tile
                bsub_ref[rows, cols] = below if (rb == 0 and cb == nb - 1) else zeros

    q = q_ref[...]

    def scores(c, kc):
        return _nt_dot(q[:, c * dh:(c + 1) * dh], kc[:, c * dh:(c + 1) * dh])

    def kv_chunk(j):
        start = pl.multiple_of(j * tq, tq)
        return k_ref[pl.ds(start, tq), :], v_ref[pl.ds(start, tq), :]

    n_rep_s, n_rep_v = tq // V7X_LANES, DIFF_V_DIM // V7X_LANES

    def update(c, s, vc):
        m_prev = m_ref[c]
        m_new = jnp.maximum(m_prev, jnp.max(s, axis=1, keepdims=True))
        scale = jnp.exp2(m_prev - m_new)
        p = jnp.exp2(s - _tile_lanes(m_new, n_rep_s))
        l_ref[c] = scale * l_ref[c] + jnp.sum(p, axis=1, keepdims=True)
        acc_ref[c] = _tile_lanes(scale, n_rep_v) * acc_ref[c] + _dot(p.astype(BF16), vc)
        m_ref[c] = m_new

    kc, vc = kv_chunk(qi)
    for c in range(2):
        s = scores(c, kc) + bdiag_ref[...]
        m = jnp.broadcast_to(jnp.max(s, axis=1, keepdims=True), (tq, V7X_LANES))
        p = jnp.exp2(s - _tile_lanes(m, n_rep_s))
        m_ref[c] = m
        l_ref[c] = jnp.broadcast_to(jnp.sum(p, axis=1, keepdims=True), (tq, V7X_LANES))
        acc_ref[c] = _dot(p.astype(BF16), vc)

    @pl.when(qi >= 1)
    def _sub_diagonal():
        kc1, vc1 = kv_chunk(qi - 1)
        for c in range(2):
            update(c, scores(c, kc1) + bsub_ref[...], vc1)

    def far_body(j, carry):
        kcj, vcj = kv_chunk(j)
        for c in range(2):
            update(c, scores(c, kcj), vcj)
        return carry

    lax.fori_loop(0, jnp.maximum(qi - 1, 0), far_body, 0)

    lam = _lambda_value(lq1_ref, lk1_ref, lq2_ref, lk2_ref, lam_init)
    w1 = _tile_lanes(1.0 / l_ref[0], n_rep_v)
    w2 = _tile_lanes(lam / l_ref[1], n_rep_v)
    o = acc_ref[0] * w1 - acc_ref[1] * w2
    o = o * lax.rsqrt(jnp.mean(o * o, axis=1, keepdims=True) + RMS_EPS) * subln_ref[...] * (1.0 - lam_init)
    o_ref[...] = o.astype(o_ref.dtype)


def _diff_attention_prompt(qd, kd, vd, lam_vecs, vtab, subln_w, *, lam_init, tq=512):
    b, t, w = qd.shape
    h = w // DIFF_V_DIM
    tq = _pick_tile(t, tq)
    vec = pl.BlockSpec((1, DIFF_HEAD_DIM), lambda bi, hi, qi: (0, 0))
    in_specs = [vec, vec, vec, vec,
                pl.BlockSpec((None, 1, V7X_LANES), lambda bi, hi, qi: (hi, 0, 0)),
                pl.BlockSpec((1, DIFF_V_DIM), lambda bi, hi, qi: (0, 0)),
                pl.BlockSpec((None, tq, DIFF_V_DIM), lambda bi, hi, qi: (bi, qi, hi)),
                pl.BlockSpec((None, t, DIFF_V_DIM), lambda bi, hi, qi: (bi, 0, hi)),
                pl.BlockSpec((None, t, DIFF_V_DIM), lambda bi, hi, qi: (bi, 0, hi))]
    kern = functools.partial(_diff_attn_kernel, tq=tq, lam_init=lam_init)
    return pl.pallas_call(
        kern, grid=(b, h, t // tq), in_specs=in_specs,
        out_specs=pl.BlockSpec((None, tq, DIFF_V_DIM), lambda bi, hi, qi: (bi, qi, hi)),
        out_shape=jax.ShapeDtypeStruct((b, t, w), BF16),
        scratch_shapes=[pltpu.VMEM((tq, tq), F32), pltpu.VMEM((tq, tq), F32),
                        pltpu.VMEM((2, tq, V7X_LANES), F32), pltpu.VMEM((2, tq, V7X_LANES), F32),
                        pltpu.VMEM((2, tq, DIFF_V_DIM), F32)],
        compiler_params=_cparams(("arbitrary", "arbitrary", "arbitrary")), name="diff_attn_prompt",
    )(*lam_vecs, vtab.reshape(h, 1, V7X_LANES), subln_w.reshape(1, DIFF_V_DIM).astype(F32), qd, kd, vd)


def _later_matrix(n):
    u = (np.arange(n)[:, None] > np.arange(n)[None, :]).astype(np.float32)
    return jnp.asarray(np.concatenate([u, u], axis=0), dtype=BF16)


def _stick_break_terms(z):
    log_beta = jnp.minimum(z, 0.0) - jnp.log2(1.0 + jnp.exp2(-jnp.abs(z)))
    return log_beta, log_beta - z


def _split_bf16(x):
    hi = x.astype(BF16)
    lo = (x - hi.astype(F32)).astype(BF16)
    return hi, lo


def _sb_attn_kernel(u2_ref, q_ref, k_ref, v_ref, o_ref, c_ref, acc_ref, *, tq, tk):
    qi = pl.program_id(2)
    n_sub = tq // tk
    q = q_ref[...]
    c_ref[...] = jnp.zeros_like(c_ref)
    acc_ref[...] = jnp.zeros_like(acc_ref)

    def process(chunks):
        c = c_ref[...]
        weights, values = [], []
        for start, diag_offset in chunks:
            kc = k_ref[pl.ds(start, tk), :]
            values.append(v_ref[pl.ds(start, tk), :])
            log_beta, log_rest = _stick_break_terms(_nt_dot(q, kc))
            if diag_offset is not None:
                ii = lax.broadcasted_iota(jnp.int32, (tq, tk), 0)
                jj = lax.broadcasted_iota(jnp.int32, (tq, tk), 1)
                mask = (jj + diag_offset) < ii
                log_rest = jnp.where(mask, log_rest, 0.0)
            hi, lo = _split_bf16(log_rest)
            later = _dot(jnp.concatenate([hi, lo], axis=1), u2_ref[...]) + _tile_lanes(c, tk // V7X_LANES)
            a = jnp.exp2(log_beta + later)
            if diag_offset is not None:
                a = jnp.where(mask, a, 0.0)
            weights.append(a.astype(BF16))
            c = c + jnp.sum(log_rest, axis=1, keepdims=True)
        c_ref[...] = c
        acc_ref[...] += _dot(jnp.concatenate(weights, axis=1), jnp.concatenate(values, axis=0))

    process([(pl.multiple_of(qi * tq + r * tk, tk), r * tk) for r in reversed(range(n_sub))])

    n_far = qi

    def far_body(j, carry):
        base = (n_far - 1 - j) * tq
        process([(pl.multiple_of(base + r * tk, tk), None) for r in reversed(range(n_sub))])
        return carry

    lax.fori_loop(0, n_far, far_body, 0)
    o_ref[...] = acc_ref[...].astype(o_ref.dtype)


def _sb_attention_prompt(qs, ks, vs, *, tq=512, tk=256):
    b, t, w = qs.shape
    h = w // SB_HEAD_DIM
    tq = _pick_tile(t, tq)
    tk = _pick_tile(tq, tk)
    in_specs = [pl.BlockSpec((2 * tk, tk), lambda bi, hi, qi: (0, 0)),
                pl.BlockSpec((None, tq, SB_HEAD_DIM), lambda bi, hi, qi: (bi, qi, hi)),
                pl.BlockSpec((None, t, SB_HEAD_DIM), lambda bi, hi, qi: (bi, 0, hi)),
                pl.BlockSpec((None, t, SB_HEAD_DIM), lambda bi, hi, qi: (bi, 0, hi))]
    kern = functools.partial(_sb_attn_kernel, tq=tq, tk=tk)
    return pl.pallas_call(
        kern, grid=(b, h, t // tq), in_specs=in_specs,
        out_specs=pl.BlockSpec((None, tq, SB_HEAD_DIM), lambda bi, hi, qi: (bi, qi, hi)),
        out_shape=jax.ShapeDtypeStruct((b, t, w), BF16),
        scratch_shapes=[pltpu.VMEM((tq, V7X_LANES), F32), pltpu.VMEM((tq, SB_HEAD_DIM), F32)],
        compiler_params=_cparams(("parallel", "parallel", "parallel")), name="sb_attn_prompt",
    )(_later_matrix(tk), qs, ks, vs)


def _expand_queries(q, n_groups, group_cols):
    r, c = q.shape
    tiled = jnp.tile(q, (n_groups, 1))
    row_g = lax.broadcasted_iota(jnp.int32, (n_groups * r, c), 0) >> _log2(r)
    col_g = lax.broadcasted_iota(jnp.int32, (n_groups * r, c), 1) >> _log2(group_cols)
    return jnp.where(row_g == col_g, tiled, 0.0).astype(BF16)


def _log2(n):
    assert n & (n - 1) == 0
    return n.bit_length() - 1


_PAGE_SLOTS = 3


def _sample_attn_kernel(pt_ref, lq1_ref, lk1_ref, lq2_ref, lk2_ref, subln_ref, btail_ref, ut_ref,
                        ed_ref, md_ref, es_ref, ms_ref,
                        qd_ref, qs_ref, knd_ref, kns_ref, vnd_ref, vns_ref, *rest,
                        pages_per_step, n_pages, n_q, n_new, lam_init):
    pps = pages_per_step
    (kd_hbm, ks_hbm, vd_hbm, vs_hbm, od_ref, os_ref,
     qexp_ref, s_ref, a_ref, accd_ref, accs_ref, kbuf, vdbuf, sem) = rest
    seq, n_seq = pl.program_id(0), pl.num_programs(0)
    n_steps = n_pages // pps
    ahead = _PAGE_SLOTS - 1

    def slot_of(key_or_value_step, is_value):
        return lax.rem(seq * (2 * n_steps) + key_or_value_step + (n_steps if is_value else 0), _PAGE_SLOTS)

    def key_copies(which_seq, step, slot):
        out = []
        for p in range(pps):
            pg = pt_ref[which_seq, step * pps + p]
            out.append(pltpu.make_async_copy(kd_hbm.at[pg], kbuf.at[slot, p], sem.at[slot]))
            out.append(pltpu.make_async_copy(ks_hbm.at[pg], kbuf.at[slot, pps + p], sem.at[slot]))
        return out

    def value_copies(which_seq, step, slot):
        out = []
        for p in range(pps):
            pg = pt_ref[which_seq, step * pps + p]
            out.append(pltpu.make_async_copy(vd_hbm.at[pg], vdbuf.at[slot, p], sem.at[slot]))
            out.append(pltpu.make_async_copy(vs_hbm.at[pg], kbuf.at[slot, p], sem.at[slot]))
        return out

    def start(copies):
        for c in copies:
            c.start()

    def wait(copies):
        for c in copies:
            c.wait()

    def prefetch(step_after, is_value):
        nxt = step_after
        slot = slot_of(nxt, is_value)
        if not is_value:
            @pl.when(nxt < n_steps)
            def _():
                start(key_copies(seq, nxt, slot))

            @pl.when(nxt >= n_steps)
            def _():
                start(value_copies(seq, nxt - n_steps, slot))
        else:
            @pl.when(nxt < n_steps)
            def _():
                start(value_copies(seq, nxt, slot))

            @pl.when(jnp.logical_and(nxt >= n_steps, seq + 1 < n_seq))
            def _():
                start(key_copies(seq + 1, nxt - n_steps, slot))

    @pl.when(seq == 0)
    def _first_pages():
        for step in range(ahead):
            start(key_copies(0, step, slot_of(step, False)))

    page = V7X_LANES
    n_past = n_pages * page
    lanes = V7X_LANES
    pair = 2 * n_q
    n_pair = lanes // pair
    n_groups = lanes // n_q

    def pad_to_page(x):
        return jnp.concatenate([x, jnp.zeros((page - n_new, x.shape[1]), x.dtype)], axis=0)

    def keys_by_group(k_ref):
        slabs = [k_ref[pl.ds(g, page, stride=n_groups), :] for g in range(n_groups)]
        return jnp.concatenate(slabs, axis=1).astype(BF16)

    qexp_ref[0] = _expand_queries(qd_ref[...], n_groups, DIFF_HEAD_DIM)
    qexp_ref[1] = _expand_queries(qs_ref[...], n_groups, SB_HEAD_DIM)
    s_ref[0, n_past:, :] = _nt_dot(knd_ref[...], qexp_ref[0])
    s_ref[1, n_past:, :] = _nt_dot(kns_ref[...], qexp_ref[1])

    def key_step(step, carry):
        prefetch(step + ahead, False)
        slot = slot_of(step, False)
        wait(key_copies(seq, step, slot))
        for p in range(pps):
            row = pl.multiple_of((step * pps + p) * page, page)
            s_ref[0, pl.ds(row, page), :] = _nt_dot(keys_by_group(kbuf.at[slot, p]), qexp_ref[0])
            s_ref[1, pl.ds(row, page), :] = _nt_dot(keys_by_group(kbuf.at[slot, pps + p]), qexp_ref[1])
        return carry

    lax.fori_loop(0, n_steps, key_step, 0)

    def attention_weights():
        lane = lax.broadcasted_iota(jnp.int32, (1, lanes), 1)
        lam = _lambda_value(lq1_ref, lk1_ref, lq2_ref, lk2_ref, lam_init)
        s_ref[0, n_past - page:, :] = s_ref[0, n_past - page:, :] + btail_ref[...]
        s = s_ref[0]
        m = jnp.max(s, axis=0, keepdims=True)
        p = jnp.exp2(s - m)
        is_first = ((lane >> _log2(n_q)) & 1) == 0
        coef = jnp.where(is_first, 1.0, -lam) / jnp.sum(p, axis=0, keepdims=True)
        for pg in range(n_pages):
            a_ref[0, pg] = jnp.transpose(p[pg * page:(pg + 1) * page] * coef).astype(BF16)
        a_ref[0, n_pages] = jnp.transpose(pad_to_page(p[n_past:] * coef)).astype(BF16)

        log_beta, log_rest = _stick_break_terms(s_ref[1])
        row = lax.broadcasted_iota(jnp.int32, (page, lanes), 0)
        new_ok = row < (lax.broadcasted_iota(jnp.int32, (page, lanes), 1) & (n_q - 1))
        ut = ut_ref[...]
        rest_new = jnp.where(new_ok, pad_to_page(log_rest[n_past:]), 0.0)
        hi, lo = _split_bf16(rest_new)
        later = _dot(ut, hi) + _dot(ut, lo)
        a_new = jnp.where(new_ok, jnp.exp2(pad_to_page(log_beta[n_past:]) + later), 0.0)
        a_ref[1, n_pages] = jnp.transpose(a_new).astype(BF16)
        carry = jnp.sum(rest_new, axis=0, keepdims=True)
        for pg in reversed(range(n_pages)):
            rows = slice(pg * page, (pg + 1) * page)
            hi, lo = _split_bf16(log_rest[rows])
            later = _dot(ut, hi) + _dot(ut, lo) + carry
            a_ref[1, pg] = jnp.transpose(jnp.exp2(log_beta[rows] + later)).astype(BF16)
            carry = carry + jnp.sum(log_rest[rows], axis=0, keepdims=True)

        a_new_d, a_new_s = a_ref[0, n_pages], a_ref[1, n_pages]
        vn_d, vn_s = pad_to_page(vnd_ref[...]), pad_to_page(vns_ref[...])
        for hp in range(n_pair):
            rows = slice(hp * pair, (hp + 1) * pair)
            cols = slice(hp * DIFF_V_DIM, (hp + 1) * DIFF_V_DIM)
            accd_ref[rows, :] = _dot(a_new_d[rows, :], vn_d[:, cols])
            both = _dot(a_new_s[rows, :], vn_s[:, cols])
            accs_ref[hp * pair:hp * pair + n_q, :] = both[:n_q, :SB_HEAD_DIM]
            accs_ref[hp * pair + n_q:(hp + 1) * pair, :] = both[n_q:, SB_HEAD_DIM:]

    attention_weights()

    def value_step(step, carry):
        prefetch(step + ahead, True)
        slot = slot_of(step, True)
        wait(value_copies(seq, step, slot))
        for p in range(pps):
            pg = step * pps + p
            for g, e_ref, m_ref, v_page, acc in ((0, ed_ref, md_ref, vdbuf.at[slot, p], accd_ref),
                                                 (1, es_ref, ms_ref, kbuf.at[slot, p], accs_ref)):
                spread = _dot(a_ref[g, pg], e_ref[...]).astype(BF16) * m_ref[...]
                acc[...] += _dot(spread, v_page[...].astype(BF16))
        return carry

    lax.fori_loop(0, n_steps, value_step, 0)

    for hp in range(n_pair):
        first = accd_ref[hp * pair:hp * pair + n_q, :]
        second = accd_ref[hp * pair + n_q:(hp + 1) * pair, :]
        o = first + second
        o = (o * lax.rsqrt(jnp.mean(o * o, axis=1, keepdims=True) + RMS_EPS)
             * subln_ref[...] * (1.0 - lam_init))
        od_ref[:, hp * DIFF_V_DIM:(hp + 1) * DIFF_V_DIM] = o.astype(od_ref.dtype)
    for h in range(n_groups):
        os_ref[:, h * SB_HEAD_DIM:(h + 1) * SB_HEAD_DIM] = accs_ref[h * n_q:(h + 1) * n_q, :].astype(os_ref.dtype)


def _sample_tail_bias(rel_bias, n_q, n_new, past_len):
    del past_len
    page = V7X_LANES
    n_rows = page + n_new
    near = _near_bias_log2(rel_bias)
    n_heads = near.shape[0]
    by_dist = jnp.concatenate([jnp.full((n_heads, n_new), NEG_INF, F32), near,
                               jnp.zeros((n_heads, n_q), F32)], axis=1)
    falling = by_dist[:, ::-1]
    per_query = [falling[:, n_q - 1 - i:n_q - 1 - i + n_rows] for i in range(n_q)]
    tail = jnp.stack(per_query, axis=-1)
    tail = jnp.broadcast_to(tail[:, :, None, :], (n_heads, n_rows, 2, n_q))
    return jnp.transpose(tail, (1, 0, 2, 3)).reshape(n_rows, n_heads * 2 * n_q)


def _sample_attention(qd, qs, knd, kns, vnd, vns, cache_kd, cache_ks, cache_vd, cache_vs, page_table,
                      lam_vecs, subln_w, rel_bias, *, lam_init, pages_per_step=4):
    bsz, n_q, w = qd.shape
    n_pages = page_table.shape[1]
    page = V7X_LANES
    n_groups = V7X_LANES // n_q
    n_heads_d = w // DIFF_V_DIM
    assert w == 8 * DIFF_V_DIM and n_q == 8
    assert cache_kd.shape[1:] == (page * n_groups, DIFF_HEAD_DIM) == cache_ks.shape[1:] == cache_vs.shape[1:]
    assert cache_vd.shape[1:] == (page * n_heads_d, DIFF_V_DIM)
    n_new = 2 * n_q
    pps = pages_per_step
    n_steps = n_pages // pps
    past_len = n_pages * page

    def pad_new(x):
        return jnp.pad(x, ((0, 0), (0, n_new - n_q), (0, 0)))

    def fixed(shape):
        return pl.BlockSpec(shape, lambda b, pt: (0,) * len(shape))

    def per_seq(rows):
        return pl.BlockSpec((None, rows, w), lambda b, pt: (b, 0, 0))

    in_hbm = pl.BlockSpec(memory_space=pl.ANY)

    def spread_and_keep(n_heads, rows_per_head):
        col = np.arange(page * n_heads)
        spread = (np.arange(page)[:, None] == col[None, :] // n_heads).astype(np.float32)
        keep = (np.arange(V7X_LANES)[:, None] // rows_per_head == col[None, :] % n_heads).astype(np.float32)
        return jnp.asarray(spread, dtype=BF16), jnp.asarray(keep, dtype=BF16)

    e_d, m_d = spread_and_keep(n_heads_d, 2 * n_q)
    e_s, m_s = spread_and_keep(n_groups, n_q)
    k_shape, vd_shape = (page * n_groups, DIFF_HEAD_DIM), (page * n_heads_d, DIFF_V_DIM)
    vec = fixed((1, DIFF_HEAD_DIM))
    in_specs = ([vec, vec, vec, vec, fixed((1, DIFF_V_DIM)), fixed((page + n_new, V7X_LANES)),
                 fixed((page, page)), fixed(e_d.shape), fixed(m_d.shape), fixed(e_s.shape), fixed(m_s.shape),
                 per_seq(n_q), per_seq(n_q),
                 per_seq(n_new), per_seq(n_new), per_seq(n_new), per_seq(n_new),
                 in_hbm, in_hbm, in_hbm, in_hbm])
    ut = jnp.asarray((np.arange(page)[None, :] > np.arange(page)[:, None]).astype(np.float32), dtype=BF16)
    args = [*lam_vecs, subln_w.reshape(1, DIFF_V_DIM).astype(F32),
            _sample_tail_bias(rel_bias, n_q, n_new, past_len), ut, e_d, m_d, e_s, m_s,
            qd, qs, pad_new(knd), pad_new(kns), pad_new(vnd), pad_new(vns),
            cache_kd, cache_ks, cache_vd, cache_vs]
    assert _PAGE_SLOTS - 1 <= n_steps and n_pages % pps == 0
    kern = functools.partial(_sample_attn_kernel, pages_per_step=pps, n_pages=n_pages, n_q=n_q,
                             n_new=n_new, lam_init=lam_init)
    n_keys = past_len + n_new
    grid_spec = pltpu.PrefetchScalarGridSpec(
        num_scalar_prefetch=1, grid=(bsz,), in_specs=in_specs,
        out_specs=[pl.BlockSpec((None, n_q, w), lambda b, pt: (b, 0, 0))] * 2,
        scratch_shapes=[pltpu.VMEM((2, V7X_LANES, w), BF16),
                        pltpu.VMEM((2, n_keys, V7X_LANES), F32),
                        pltpu.VMEM((2, n_pages + 1, V7X_LANES, page), BF16),
                        pltpu.VMEM((V7X_LANES, DIFF_V_DIM), F32),
                        pltpu.VMEM((V7X_LANES, SB_HEAD_DIM), F32),
                        pltpu.VMEM((_PAGE_SLOTS, 2 * pps) + k_shape, F32),
                        pltpu.VMEM((_PAGE_SLOTS, pps) + vd_shape, F32),
                        pltpu.SemaphoreType.DMA((_PAGE_SLOTS,))])
    return pl.pallas_call(
        kern, grid_spec=grid_spec,
        out_shape=[jax.ShapeDtypeStruct((bsz, n_q, w), F32)] * 2,
        compiler_params=_cparams(("arbitrary",)), name="sample_attn",
    )(page_table, *args)


def _cross_attn_prompt_kernel(q_ref, k_ref, v_ref, o_ref):
    s = _nt_dot(q_ref[...], k_ref[...])
    p = jnp.exp2(s - jnp.max(s, axis=1, keepdims=True))
    o = _dot(p.astype(BF16), v_ref[...]) / jnp.sum(p, axis=1, keepdims=True)
    o_ref[...] = o.astype(o_ref.dtype)


def _cross_attention_prompt(q, mk, mv, *, tq=1024):
    b, t, d = q.shape
    mem_len = mk.shape[1]
    hd = d // MEM_HEADS
    tq = _pick_tile(t, tq)
    q_spec = pl.BlockSpec((None, tq, hd), lambda bi, qi, hi: (bi, qi, hi))
    m_spec = pl.BlockSpec((None, mem_len, hd), lambda bi, qi, hi: (bi, 0, hi))
    return pl.pallas_call(
        _cross_attn_prompt_kernel, grid=(b, t // tq, MEM_HEADS), in_specs=[q_spec, m_spec, m_spec],
        out_specs=q_spec, out_shape=jax.ShapeDtypeStruct((b, t, d), BF16),
        compiler_params=_cparams(("parallel", "parallel", "parallel")), name="cross_attn_prompt",
    )(q, mk, mv)


def _cross_attn_sample_kernel(q_ref, k_ref, v_ref, o_ref, *, n_q):
    mem_len, n_heads, hd = k_ref.shape
    rows, lanes = mem_len * n_heads, V7X_LANES
    q = q_ref[...]
    q_by_head = jnp.concatenate([q[:, h * hd:(h + 1) * hd] for h in range(n_heads)]
                                + [jnp.zeros((lanes - n_heads * n_q, hd), F32)], axis=0)
    k = k_ref[...].reshape(rows, hd).astype(BF16)
    s = _nt_dot(k, q_by_head.astype(BF16))
    row_head = lax.broadcasted_iota(jnp.int32, (rows, lanes), 0) & (n_heads - 1)
    lane = lax.broadcasted_iota(jnp.int32, (rows, lanes), 1)
    s = jnp.where(row_head == (lane >> _log2(n_q)), s, NEG_INF)
    p = jnp.exp2(s - jnp.max(s, axis=0, keepdims=True))
    weights = jnp.transpose(p / jnp.sum(p, axis=0, keepdims=True)).astype(BF16)
    o = _dot(weights, v_ref[...].reshape(rows, hd).astype(BF16))
    for h in range(n_heads):
        o_ref[:, h * hd:(h + 1) * hd] = o[h * n_q:(h + 1) * n_q].astype(o_ref.dtype)


def _cross_attention_sample(q, mem_k, mem_v):
    b, n_q, d = q.shape
    mem_len = mem_k.shape[1]
    q_spec = pl.BlockSpec((None, n_q, d), lambda bi: (bi, 0, 0))
    m_spec = pl.BlockSpec((None, mem_len, MEM_HEADS, d // MEM_HEADS), lambda bi: (bi, 0, 0, 0))
    return pl.pallas_call(
        functools.partial(_cross_attn_sample_kernel, n_q=n_q), grid=(b,),
        in_specs=[q_spec, m_spec, m_spec], out_specs=q_spec,
        out_shape=jax.ShapeDtypeStruct((b, n_q, d), F32),
        compiler_params=_cparams(("parallel",)), name="cross_attn_sample",
    )(q, mem_k, mem_v)


def _post_attention(x, mixed, cross_fn, wts, alpha, q_dtype):
    (w_out, ln1_g, ln1_b, w_cq, w_co, ln2_g, ln2_b, w_up, b_up, w_down, b_down, ln3_g, ln3_b) = wts
    d = x.shape[1]
    (y,) = _matmul(mixed, w_out, out_dtypes=[F32], res=x, alpha=alpha, name="mm_out")
    x1, x1b = _layer_norm(y, ln1_g, ln1_b, name="ln1")
    (qc,) = _matmul(x1b, w_cq, out_dtypes=[q_dtype], out_scale=(d // MEM_HEADS) ** -0.5 * LOG2E, name="mm_cq")
    oc = cross_fn(qc).astype(BF16)
    (y,) = _matmul(oc, w_co, out_dtypes=[F32], res=x1, alpha=alpha, name="mm_co")
    x2, x2b = _layer_norm(y, ln2_g, ln2_b, name="ln2")
    (hid,) = _matmul(x2b, w_up, out_dtypes=[BF16], bias=b_up, act="relu2", name="mm_up")
    (y,) = _matmul(hid, w_down, out_dtypes=[F32], bias=b_down, res=x2, alpha=alpha, tk=2048, name="mm_down")
    x3, _ = _layer_norm(y, ln3_g, ln3_b, name="ln3")
    return x3


def _project_qkv(xb, w_in_b, q_scale, q_dtype):
    dw = w_in_b.shape[1] // 6
    sec = [(i * dw, dw) for i in range(6)]
    (qd,) = _matmul(xb, w_in_b, cols=sec[0], out_dtypes=[q_dtype], out_scale=q_scale, name="mm_qd")
    kd32, kd = _matmul(xb, w_in_b, cols=sec[1], out_dtypes=[F32, BF16], name="mm_kd")
    vd32, vd = _matmul(xb, w_in_b, cols=sec[2], out_dtypes=[F32, BF16], name="mm_vd")
    (qs,) = _matmul(xb, w_in_b, cols=sec[3], out_dtypes=[q_dtype], out_scale=q_scale, name="mm_qs")
    ks32, ks = _matmul(xb, w_in_b, cols=sec[4], out_dtypes=[F32, BF16], name="mm_ks")
    vs32, vs = _matmul(xb, w_in_b, cols=sec[5], out_dtypes=[F32, BF16], name="mm_vs")
    return (qd, kd, vd, qs, ks, vs), (kd32, vd32, ks32, vs32)


def kernel(x_prompt, x_sample, mem_prompt, cache_diff_k, cache_diff_v, cache_sb_k, cache_sb_v, cache_mem_k, cache_mem_v, page_table, w_in, lambda_q1, lambda_k1, lambda_q2, lambda_k2, subln_w, rel_bias, w_out, ln1_g, ln1_b, w_cq, w_ck, w_cv, w_co, ln2_g, ln2_b, w_up, b_up, w_down, b_down, ln3_g, ln3_b):
    depth = w_in.shape[0]
    assert depth == 1, "single-layer step"
    bsz, seq, d = x_prompt.shape
    dec_b, dec_t, _ = x_sample.shape
    mem_len = mem_prompt.shape[1]
    n_pool, page = cache_diff_k.shape[1], cache_diff_k.shape[2]
    dw = d // 2
    alpha = (2.0 * depth) ** 0.25
    l = 0
    lam_init = 0.8 - 0.6 * math.exp(-0.3 * l)
    q_scale = DIFF_HEAD_DIM ** -0.5 * LOG2E
    assert DIFF_HEAD_DIM == SB_HEAD_DIM

    w_in_b = w_in[l].astype(BF16)
    tail_w = (w_out[l].astype(BF16), ln1_g[l], ln1_b[l], w_cq[l].astype(BF16), w_co[l].astype(BF16),
              ln2_g[l], ln2_b[l], w_up[l].astype(BF16), b_up[l], w_down[l].astype(BF16), b_down[l],
              ln3_g[l], ln3_b[l])
    lam_vecs = [v[l].reshape(1, DIFF_HEAD_DIM).astype(F32) for v in (lambda_q1, lambda_k1, lambda_q2, lambda_k2)]
    vnear = _near_bias_log2(rel_bias)
    vtab = jnp.concatenate([vnear[:, :1], vnear[:, :0:-1]], axis=1)

    xp = x_prompt.reshape(bsz * seq, d)
    (qd, kd, vd, qs, ks, vs), (kd32, vd32, ks32, vs32) = _project_qkv(xp.astype(BF16), w_in_b, q_scale,
                                                                      BF16)
    shape3 = (bsz, seq, dw)
    mixed_d = _diff_attention_prompt(qd.reshape(shape3), kd.reshape(shape3), vd.reshape(shape3), lam_vecs,
                                     vtab, subln_w[l], lam_init=lam_init)
    mixed_s = _sb_attention_prompt(qs.reshape(shape3), ks.reshape(shape3), vs.reshape(shape3))
    mixed = jnp.concatenate([mixed_d, mixed_s], axis=-1).reshape(bsz * seq, d)
    memb = mem_prompt.reshape(bsz * mem_len, d).astype(BF16)
    mk32, mk = _matmul(memb, w_ck[l].astype(BF16), out_dtypes=[F32, BF16], name="mm_ck")
    mv32, mv = _matmul(memb, w_cv[l].astype(BF16), out_dtypes=[F32, BF16], name="mm_cv")

    def cross_prompt(qc):
        o = _cross_attention_prompt(qc.reshape(bsz, seq, d), mk.reshape(bsz, mem_len, d),
                                    mv.reshape(bsz, mem_len, d))
        return o.reshape(bsz * seq, d)

    y_prompt = _post_attention(xp, mixed, cross_prompt, tail_w, alpha, BF16).reshape(bsz, seq, d)

    xs = x_sample.reshape(dec_b * dec_t, d)
    (sqd, skd, svd, sqs, sks, svs), (skd32, svd32, sks32, svs32) = _project_qkv(xs.astype(BF16), w_in_b,
                                                                              q_scale, F32)
    s3 = (dec_b, dec_t, dw)
    smix_d, smix_s = _sample_attention(
        sqd.reshape(s3), sqs.reshape(s3), skd.reshape(s3), sks.reshape(s3), svd.reshape(s3), svs.reshape(s3),
        cache_diff_k[l].reshape(n_pool, -1, DIFF_HEAD_DIM), cache_sb_k[l].reshape(n_pool, -1, SB_HEAD_DIM),
        cache_diff_v[l].reshape(n_pool, -1, DIFF_V_DIM), cache_sb_v[l].reshape(n_pool, -1, SB_HEAD_DIM),
        page_table, lam_vecs, subln_w[l], rel_bias, lam_init=lam_init)
    smixed = jnp.concatenate([smix_d, smix_s], axis=-1).reshape(dec_b * dec_t, d).astype(BF16)

    def cross_sample(qc):
        o = _cross_attention_sample(qc.reshape(dec_b, dec_t, d), cache_mem_k[l], cache_mem_v[l])
        return o.reshape(dec_b * dec_t, d)

    y_sample = _post_attention(xs, smixed, cross_sample, tail_w, alpha, F32).reshape(dec_b, dec_t, d)

    dh, hd = DIFF_HEAD_DIM, dw // DIFF_V_DIM
    hs = dw // SB_HEAD_DIM
    mh = d // MEM_HEADS
    return (y_prompt, y_sample,
            kd32.reshape(1, bsz, seq, hd, 2, dh), vd32.reshape(1, bsz, seq, hd, 2 * dh),
            ks32.reshape(1, bsz, seq, hs, SB_HEAD_DIM), vs32.reshape(1, bsz, seq, hs, SB_HEAD_DIM),
            mk32.reshape(1, bsz, mem_len, MEM_HEADS, mh), mv32.reshape(1, bsz, mem_len, MEM_HEADS, mh),
            skd32.reshape(1, dec_b, dec_t, hd, 2, dh), svd32.reshape(1, dec_b, dec_t, hd, 2 * dh),
            sks32.reshape(1, dec_b, dec_t, hs, SB_HEAD_DIM), svs32.reshape(1, dec_b, dec_t, hs, SB_HEAD_DIM))
```

```python
import functools
import math

import numpy as np
import jax
import jax.numpy as jnp
from jax import lax
from jax.experimental import pallas as pl
from jax.experimental.pallas import tpu as pltpu

F32 = jnp.float32
BF16 = jnp.bfloat16

DIFF_HEAD_DIM = 128
DIFF_V_DIM = 2 * DIFF_HEAD_DIM
SB_HEAD_DIM = 128
MEM_HEADS = 4
N_BUCKETS = 32
MAX_DISTANCE = 128
LN_EPS = 1e-5
RMS_EPS = 1e-5
NEG_INF = -1e30
LOG2E = math.log2(math.e)

V7X_LANES = 128
V7X_VMEM_BYTES = 64 * 1024 * 1024
VMEM_LIMIT_BYTES = V7X_VMEM_BYTES - 8 * 1024 * 1024


def _cparams(semantics):
    return pltpu.CompilerParams(dimension_semantics=semantics, vmem_limit_bytes=VMEM_LIMIT_BYTES)


def _nt_dot(a, b):
    return lax.dot_general(a, b, (((1,), (1,)), ((), ())), preferred_element_type=F32)


def _dot(a, b):
    return jnp.dot(a, b, preferred_element_type=F32)


def _tile_lanes(x, n):
    return x if n == 1 else jnp.concatenate([x] * n, axis=1)


def _mm_kernel(*refs, nk, has_bias, has_res, act, alpha, out_scale, n_out, emit_w):
    x_ref, w_ref = refs[0], refs[1]
    pos = 2
    bias_ref = res_ref = w_out_ref = None
    if has_bias:
        bias_ref = refs[pos]
        pos += 1
    if has_res:
        res_ref = refs[pos]
        pos += 1
    out_refs = refs[pos:pos + n_out]
    pos += n_out
    if emit_w:
        w_out_ref = refs[pos]
        pos += 1
    acc_ref = refs[pos] if nk > 1 else None

    def epilogue(y):
        if has_bias:
            y = y + bias_ref[...]
        if act == "relu2":
            y = jnp.square(jnp.maximum(y, 0.0))
        if has_res:
            y = alpha * res_ref[...] + y
        if out_scale != 1.0:
            y = y * out_scale
        for o_ref in out_refs:
            o_ref[...] = y.astype(o_ref.dtype)

    w = w_ref[...].astype(BF16)
    if emit_w:
        w_out_ref[...] = w
    part = _dot(x_ref[...], w)
    if nk == 1:
        epilogue(part)
    else:
        k = pl.program_id(2)

        @pl.when(k == 0)
        def _():
            acc_ref[...] = part

        @pl.when(jnp.logical_and(k > 0, k < nk - 1))
        def _():
            acc_ref[...] += part

        @pl.when(k == nk - 1)
        def _():
            epilogue(acc_ref[...] + part)


def _pick_tile(n, pref):
    t = min(n, pref)
    while n % t:
        t //= 2
    return t


def _matmul(x, w, *, out_dtypes, bias=None, res=None, act=None, alpha=1.0, out_scale=1.0,
            tm=1024, tn=1024, tk=4096, cols=None, emit_w=False, name="mm"):
    m, kdim = x.shape
    col0, n = cols if cols is not None else (0, w.shape[1])
    if w.dtype == F32:
        tm, tn, tk = m, min(tn, 512), min(tk, 2048)
    tm, tn, tk = _pick_tile(m, tm), _pick_tile(n, tn), _pick_tile(kdim, tk)
    assert col0 % tn == 0 and (w.dtype == BF16 or m == tm) and (w.dtype == F32 or not emit_w)
    j0 = col0 // tn
    nk = kdim // tk
    in_specs = [pl.BlockSpec((tm, tk), lambda i, j, k: (i, k)),
                pl.BlockSpec((tk, tn), lambda i, j, k: (k, j + j0))]
    args = [x, w]
    if bias is not None:
        in_specs.append(pl.BlockSpec((1, tn), lambda i, j, k: (0, j)))
        args.append(bias.reshape(1, n).astype(F32))
    if res is not None:
        in_specs.append(pl.BlockSpec((tm, tn), lambda i, j, k: (i, j)))
        args.append(res)
    out_specs = [pl.BlockSpec((tm, tn), lambda i, j, k: (i, j)) for _ in out_dtypes]
    out_shape = [jax.ShapeDtypeStruct((m, n), dt) for dt in out_dtypes]
    if emit_w:
        out_specs.append(pl.BlockSpec((tk, tn), lambda i, j, k: (k, j)))
        out_shape.append(jax.ShapeDtypeStruct((kdim, n), BF16))
    scratch = [pltpu.VMEM((tm, tn), F32)] if nk > 1 else []
    kern = functools.partial(_mm_kernel, nk=nk, has_bias=bias is not None, has_res=res is not None,
                             act=act, alpha=alpha, out_scale=out_scale, n_out=len(out_dtypes), emit_w=emit_w)
    return pl.pallas_call(
        kern, grid=(m // tm, n // tn, nk), in_specs=in_specs, out_specs=out_specs,
        out_shape=out_shape, scratch_shapes=scratch,
        compiler_params=_cparams(("parallel", "parallel", "arbitrary")), name=name,
    )(*args)


def _ln_kernel(y_ref, g_ref, b_ref, o32_ref, o16_ref):
    y = y_ref[...]
    mu = jnp.mean(y, axis=-1, keepdims=True)
    d = y - mu
    var = jnp.mean(jnp.square(d), axis=-1, keepdims=True)
    o = d * lax.rsqrt(var + LN_EPS) * g_ref[...] + b_ref[...]
    o32_ref[...] = o
    o16_ref[...] = o.astype(BF16)


def _layer_norm(y, g, b, *, tm=256, name="ln"):
    m, d = y.shape
    tm = _pick_tile(m, tm)
    row = pl.BlockSpec((tm, d), lambda i: (i, 0))
    vec = pl.BlockSpec((1, d), lambda i: (0, 0))
    return pl.pallas_call(
        _ln_kernel, grid=(m // tm,), in_specs=[row, vec, vec], out_specs=[row, row],
        out_shape=[jax.ShapeDtypeStruct((m, d), F32), jax.ShapeDtypeStruct((m, d), BF16)],
        compiler_params=_cparams(("parallel",)), name=name,
    )(y, g.reshape(1, d).astype(F32), b.reshape(1, d).astype(F32))


def _bucket_table(n):
    max_exact = N_BUCKETS // 2
    d = np.arange(n)
    df = np.maximum(d, 1).astype(np.float32)
    ratio = np.log(df / np.float32(max_exact)) / np.float32(math.log(MAX_DISTANCE / max_exact))
    large = max_exact + (ratio * np.float32(N_BUCKETS - max_exact)).astype(np.int32)
    large = np.minimum(large, N_BUCKETS - 1)
    return np.where(d < max_exact, d, large).astype(np.int32)


def _near_bias_log2(rel_bias):
    tab = rel_bias.astype(F32)[_bucket_table(MAX_DISTANCE)]
    far = rel_bias.astype(F32)[N_BUCKETS - 1]
    return jnp.transpose(tab - far[None, :]) * LOG2E


def _lambda_value(lq1_ref, lk1_ref, lq2_ref, lk2_ref, lam_init):
    s1 = jnp.sum(lq1_ref[...] * lk1_ref[...], axis=1, keepdims=True)
    s2 = jnp.sum(lq2_ref[...] * lk2_ref[...], axis=1, keepdims=True)
    return jnp.exp(s1) - jnp.exp(s2) + lam_init


def _diff_attn_kernel(lq1_ref, lk1_ref, lq2_ref, lk2_ref, vtab_ref, subln_ref, q_ref, k_ref, v_ref,
                      o_ref, bdiag_ref, bsub_ref, m_ref, l_ref, acc_ref, *, tq, tk, lam_init):
    qi = pl.program_id(2)
    dh = DIFF_HEAD_DIM
    nb = tq // V7X_LANES

    @pl.when(qi == 0)
    def _build_bias_tiles():
        toep = pltpu.roll(jnp.broadcast_to(vtab_ref[...], (V7X_LANES, V7X_LANES)), 0, 1,
                          stride=1, stride_axis=0)
        ii = lax.broadcasted_iota(jnp.int32, (V7X_LANES, V7X_LANES), 0)
        jj = lax.broadcasted_iota(jnp.int32, (V7X_LANES, V7X_LANES), 1)
        on_diag = jnp.where(ii >= jj, toep, NEG_INF)
        below = jnp.where(jj > ii, toep, 0.0)
        zeros = jnp.zeros((V7X_LANES, V7X_LANES), F32)
        masked = jnp.full((V7X_LANES, V7X_LANES), NEG_INF, F32)
        for rb in range(nb):
            for cb in range(nb):
                rows = slice(rb * V7X_LANES, (rb + 1) * V7X_LANES)
                cols = slice(cb * V7X_LANES, (cb + 1) * V7X_LANES)
                if rb == cb:
                    tile = on_diag
                elif rb == cb + 1:
                    tile = below
                elif rb > cb:
                    tile = zeros
                else:
                    tile = masked
                bdiag_ref[rows, cols] = tile
        nbk = tk // V7X_LANES
        for rb in range(nb):
            for cb in range(nbk):
                rows = slice(rb * V7X_LANES, (rb + 1) * V7X_LANES)
                cols = slice(cb * V7X_LANES, (cb + 1) * V7X_LANES)
                bsub_ref[rows, cols] = below if (rb == 0 and cb == nbk - 1) else zeros

    q = q_ref[...]

    def scores(c, kc):
        return _nt_dot(q[:, c * dh:(c + 1) * dh], kc[:, c * dh:(c + 1) * dh])

    n_sub = tq // tk
    n_rep_s, n_rep_v = tk // V7X_LANES, DIFF_V_DIM // V7X_LANES

    def kv_chunk(block, r):
        start = pl.multiple_of(block * tq + r * tk, tk)
        return k_ref[pl.ds(start, tk), :], v_ref[pl.ds(start, tk), :]

    def update(c, s, vc, row0=0):
        rows = slice(row0, tq)
        m_prev = m_ref[c, rows, :]
        m_new = jnp.maximum(m_prev, jnp.max(s, axis=1, keepdims=True))
        scale = jnp.exp2(m_prev - m_new)
        p = jnp.exp2(s - _tile_lanes(m_new, n_rep_s))
        l_ref[c, rows, :] = scale * l_ref[c, rows, :] + jnp.sum(p, axis=1, keepdims=True)
        acc_ref[c, rows, :] = _tile_lanes(scale, n_rep_v) * acc_ref[c, rows, :] + _dot(p.astype(BF16), vc)
        m_ref[c, rows, :] = m_new

    for r in range(n_sub):
        kc, vc = kv_chunk(qi, r)
        for c in range(2):
            if r > 0:
                s = _nt_dot(q[r * tk:, c * dh:(c + 1) * dh], kc[:, c * dh:(c + 1) * dh])
                update(c, s + bdiag_ref[r * tk:, r * tk:(r + 1) * tk], vc, row0=r * tk)
                continue
            s = scores(c, kc) + bdiag_ref[:, :tk]
            m = jnp.broadcast_to(jnp.max(s, axis=1, keepdims=True), (tq, V7X_LANES))
            p = jnp.exp2(s - _tile_lanes(m, n_rep_s))
            m_ref[c] = m
            l_ref[c] = jnp.broadcast_to(jnp.sum(p, axis=1, keepdims=True), (tq, V7X_LANES))
            acc_ref[c] = _dot(p.astype(BF16), vc)

    @pl.when(qi >= 1)
    def _sub_diagonal():
        for r in range(n_sub):
            kc1, vc1 = kv_chunk(qi - 1, r)
            for c in range(2):
                s = scores(c, kc1)
                if r == n_sub - 1:
                    s = s + bsub_ref[...]
                update(c, s, vc1)

    def far_body(j, carry):
        for r in range(n_sub):
            kcj, vcj = kv_chunk(j, r)
            for c in range(2):
                update(c, scores(c, kcj), vcj)
        return carry

    lax.fori_loop(0, jnp.maximum(qi - 1, 0), far_body, 0)

    lam = _lambda_value(lq1_ref, lk1_ref, lq2_ref, lk2_ref, lam_init)
    w1 = _tile_lanes(1.0 / l_ref[0], n_rep_v)
    w2 = _tile_lanes(lam / l_ref[1], n_rep_v)
    o = acc_ref[0] * w1 - acc_ref[1] * w2
    o = o * lax.rsqrt(jnp.mean(o * o, axis=1, keepdims=True) + RMS_EPS) * subln_ref[...] * (1.0 - lam_init)
    o_ref[...] = o.astype(o_ref.dtype)


def _diff_attention_prompt(qd, kd, vd, lam_vecs, vtab, subln_w, *, lam_init, tq=1024, tk=256):
    b, t, w = qd.shape
    h = w // DIFF_V_DIM
    tq = _pick_tile(t, tq)
    tk = _pick_tile(tq, tk)
    vec = pl.BlockSpec((1, DIFF_HEAD_DIM), lambda bi, hi, qi: (0, 0))
    in_specs = [vec, vec, vec, vec,
                pl.BlockSpec((None, 1, V7X_LANES), lambda bi, hi, qi: (hi, 0, 0)),
                pl.BlockSpec((1, DIFF_V_DIM), lambda bi, hi, qi: (0, 0)),
                pl.BlockSpec((None, tq, DIFF_V_DIM), lambda bi, hi, qi: (bi, qi, hi)),
                pl.BlockSpec((None, t, DIFF_V_DIM), lambda bi, hi, qi: (bi, 0, hi)),
                pl.BlockSpec((None, t, DIFF_V_DIM), lambda bi, hi, qi: (bi, 0, hi))]
    kern = functools.partial(_diff_attn_kernel, tq=tq, tk=tk, lam_init=lam_init)
    return pl.pallas_call(
        kern, grid=(b, h, t // tq), in_specs=in_specs,
        out_specs=pl.BlockSpec((None, tq, DIFF_V_DIM), lambda bi, hi, qi: (bi, qi, hi)),
        out_shape=jax.ShapeDtypeStruct((b, t, w), BF16),
        scratch_shapes=[pltpu.VMEM((tq, tq), F32), pltpu.VMEM((tq, tk), F32),
                        pltpu.VMEM((2, tq, V7X_LANES), F32), pltpu.VMEM((2, tq, V7X_LANES), F32),
                        pltpu.VMEM((2, tq, DIFF_V_DIM), F32)],
        compiler_params=_cparams(("arbitrary", "arbitrary", "arbitrary")), name="diff_attn_prompt",
    )(*lam_vecs, vtab.reshape(h, 1, V7X_LANES), subln_w.reshape(1, DIFF_V_DIM).astype(F32), qd, kd, vd)


def _later_matrix(n):
    u = (np.arange(n)[:, None] > np.arange(n)[None, :]).astype(np.float32)
    return jnp.asarray(np.concatenate([u, u], axis=0), dtype=BF16)


def _stick_break_terms(z):
    log_beta = jnp.minimum(z, 0.0) - jnp.log2(1.0 + jnp.exp2(-jnp.abs(z)))
    return log_beta, log_beta - z


def _split_bf16(x):
    hi = x.astype(BF16)
    lo = (x - hi.astype(F32)).astype(BF16)
    return hi, lo


def _sb_attn_kernel(u2_ref, q_ref, k_ref, v_ref, o_ref, c_ref, acc_ref, *, tq, tk):
    qi = pl.program_id(2)
    n_sub = tq // tk
    q = q_ref[...]
    c_ref[...] = jnp.zeros_like(c_ref)
    acc_ref[...] = jnp.zeros_like(acc_ref)

    def process(chunks, row0=0):
        rows = slice(row0, tq)
        c = c_ref[rows, :]
        weights, values = [], []
        for start, diag_offset in chunks:
            kc = k_ref[pl.ds(start, tk), :]
            values.append(v_ref[pl.ds(start, tk), :])
            log_beta, log_rest = _stick_break_terms(_nt_dot(q[row0:], kc))
            if diag_offset is not None:
                ii = lax.broadcasted_iota(jnp.int32, (tq - row0, tk), 0) + row0
                jj = lax.broadcasted_iota(jnp.int32, (tq - row0, tk), 1)
                mask = (jj + diag_offset) < ii
                log_rest = jnp.where(mask, log_rest, 0.0)
            hi, lo = _split_bf16(log_rest)
            later = _dot(jnp.concatenate([hi, lo], axis=1), u2_ref[...]) + _tile_lanes(c, tk // V7X_LANES)
            a = jnp.exp2(log_beta + later)
            if diag_offset is not None:
                a = jnp.where(mask, a, 0.0)
            weights.append(a.astype(BF16))
            c = c + jnp.sum(log_rest, axis=1, keepdims=True)
        c_ref[rows, :] = c
        acc_ref[rows, :] += _dot(jnp.concatenate(weights, axis=1), jnp.concatenate(values, axis=0))

    for r in reversed(range(n_sub)):
        process([(pl.multiple_of(qi * tq + r * tk, tk), r * tk)], row0=r * tk)

    n_far = qi

    def far_body(j, carry):
        base = (n_far - 1 - j) * tq
        process([(pl.multiple_of(base + r * tk, tk), None) for r in reversed(range(n_sub))])
        return carry

    lax.fori_loop(0, n_far, far_body, 0)
    o_ref[...] = acc_ref[...].astype(o_ref.dtype)


def _sb_attention_prompt(qs, ks, vs, *, tq=1024, tk=256):
    b, t, w = qs.shape
    h = w // SB_HEAD_DIM
    tq = _pick_tile(t, tq)
    tk = _pick_tile(tq, tk)
    in_specs = [pl.BlockSpec((2 * tk, tk), lambda bi, hi, qi: (0, 0)),
                pl.BlockSpec((None, tq, SB_HEAD_DIM), lambda bi, hi, qi: (bi, qi, hi)),
                pl.BlockSpec((None, t, SB_HEAD_DIM), lambda bi, hi, qi: (bi, 0, hi)),
                pl.BlockSpec((None, t, SB_HEAD_DIM), lambda bi, hi, qi: (bi, 0, hi))]
    kern = functools.partial(_sb_attn_kernel, tq=tq, tk=tk)
    return pl.pallas_call(
        kern, grid=(b, h, t // tq), in_specs=in_specs,
        out_specs=pl.BlockSpec((None, tq, SB_HEAD_DIM), lambda bi, hi, qi: (bi, qi, hi)),
        out_shape=jax.ShapeDtypeStruct((b, t, w), BF16),
        scratch_shapes=[pltpu.VMEM((tq, V7X_LANES), F32), pltpu.VMEM((tq, SB_HEAD_DIM), F32)],
        compiler_params=_cparams(("parallel", "parallel", "parallel")), name="sb_attn_prompt",
    )(_later_matrix(tk), qs, ks, vs)


def _expand_queries(q, n_groups, group_cols):
    r, c = q.shape
    tiled = jnp.tile(q, (n_groups, 1))
    row_g = lax.broadcasted_iota(jnp.int32, (n_groups * r, c), 0) >> _log2(r)
    col_g = lax.broadcasted_iota(jnp.int32, (n_groups * r, c), 1) >> _log2(group_cols)
    return jnp.where(row_g == col_g, tiled, 0.0).astype(BF16)


def _log2(n):
    assert n & (n - 1) == 0
    return n.bit_length() - 1


_PAGE_SLOTS = 3


def _sample_attn_kernel(pt_ref, lq1_ref, lk1_ref, lq2_ref, lk2_ref, subln_ref, btail_ref, ut_ref,
                        ed_ref, md_ref, es_ref, ms_ref,
                        qd_ref, qs_ref, knd_ref, kns_ref, vnd_ref, vns_ref, *rest,
                        pages_per_step, n_pages, n_q, n_new, lam_init):
    pps = pages_per_step
    (kd_hbm, ks_hbm, vd_hbm, vs_hbm, od_ref, os_ref,
     qexp_ref, s_ref, a_ref, accd_ref, accs_ref, kbuf, vdbuf, sem) = rest
    seq, n_seq = pl.program_id(0), pl.num_programs(0)
    n_steps = n_pages // pps
    ahead = _PAGE_SLOTS - 1

    def slot_of(key_or_value_step, is_value):
        return lax.rem(seq * (2 * n_steps) + key_or_value_step + (n_steps if is_value else 0), _PAGE_SLOTS)

    def key_copies(which_seq, step, slot):
        out = []
        for p in range(pps):
            pg = pt_ref[which_seq, step * pps + p]
            out.append(pltpu.make_async_copy(kd_hbm.at[pg], kbuf.at[slot, p], sem.at[slot]))
            out.append(pltpu.make_async_copy(ks_hbm.at[pg], kbuf.at[slot, pps + p], sem.at[slot]))
        return out

    def value_copies(which_seq, step, slot):
        out = []
        for p in range(pps):
            pg = pt_ref[which_seq, step * pps + p]
            out.append(pltpu.make_async_copy(vd_hbm.at[pg], vdbuf.at[slot, p], sem.at[slot]))
            out.append(pltpu.make_async_copy(vs_hbm.at[pg], kbuf.at[slot, p], sem.at[slot]))
        return out

    def start(copies):
        for c in copies:
            c.start()

    def wait(copies):
        for c in copies:
            c.wait()

    def prefetch(step_after, is_value):
        nxt = step_after
        slot = slot_of(nxt, is_value)
        if not is_value:
            @pl.when(nxt < n_steps)
            def _():
                start(key_copies(seq, nxt, slot))

            @pl.when(nxt >= n_steps)
            def _():
                start(value_copies(seq, nxt - n_steps, slot))
        else:
            @pl.when(nxt < n_steps)
            def _():
                start(value_copies(seq, nxt, slot))

            @pl.when(jnp.logical_and(nxt >= n_steps, seq + 1 < n_seq))
            def _():
                start(key_copies(seq + 1, nxt - n_steps, slot))

    @pl.when(seq == 0)
    def _first_pages():
        for step in range(ahead):
            start(key_copies(0, step, slot_of(step, False)))

    page = V7X_LANES
    n_past = n_pages * page
    lanes = V7X_LANES
    pair = 2 * n_q
    n_pair = lanes // pair
    n_groups = lanes // n_q

    def pad_to_page(x):
        return jnp.concatenate([x, jnp.zeros((page - n_new, x.shape[1]), x.dtype)], axis=0)

    def keys_by_group(k_ref):
        slabs = [k_ref[pl.ds(g, page, stride=n_groups), :] for g in range(n_groups)]
        return jnp.concatenate(slabs, axis=1).astype(BF16)

    qexp_ref[0] = _expand_queries(qd_ref[...], n_groups, DIFF_HEAD_DIM)
    qexp_ref[1] = _expand_queries(qs_ref[...], n_groups, SB_HEAD_DIM)
    s_ref[0, n_past:, :] = _nt_dot(knd_ref[...], qexp_ref[0])
    s_ref[1, n_past:, :] = _nt_dot(kns_ref[...], qexp_ref[1])

    def key_step(step, carry):
        prefetch(step + ahead, False)
        slot = slot_of(step, False)
        wait(key_copies(seq, step, slot))
        for p in range(pps):
            row = pl.multiple_of((step * pps + p) * page, page)
            s_ref[0, pl.ds(row, page), :] = _nt_dot(keys_by_group(kbuf.at[slot, p]), qexp_ref[0])
            s_ref[1, pl.ds(row, page), :] = _nt_dot(keys_by_group(kbuf.at[slot, pps + p]), qexp_ref[1])
        return carry

    lax.fori_loop(0, n_steps, key_step, 0)

    def attention_weights():
        lane = lax.broadcasted_iota(jnp.int32, (1, lanes), 1)
        lam = _lambda_value(lq1_ref, lk1_ref, lq2_ref, lk2_ref, lam_init)
        s_ref[0, n_past - page:, :] = s_ref[0, n_past - page:, :] + btail_ref[...]
        s = s_ref[0]
        m = jnp.max(s, axis=0, keepdims=True)
        p = jnp.exp2(s - m)
        is_first = ((lane >> _log2(n_q)) & 1) == 0
        coef = jnp.where(is_first, 1.0, -lam) / jnp.sum(p, axis=0, keepdims=True)
        for pg in range(n_pages):
            a_ref[0, pg] = jnp.transpose(p[pg * page:(pg + 1) * page] * coef).astype(BF16)
        a_ref[0, n_pages] = jnp.transpose(pad_to_page(p[n_past:] * coef)).astype(BF16)

        log_beta, log_rest = _stick_break_terms(s_ref[1])
        row = lax.broadcasted_iota(jnp.int32, (page, lanes), 0)
        new_ok = row < (lax.broadcasted_iota(jnp.int32, (page, lanes), 1) & (n_q - 1))
        ut = ut_ref[...]
        rest_new = jnp.where(new_ok, pad_to_page(log_rest[n_past:]), 0.0)
        hi, lo = _split_bf16(rest_new)
        later = _dot(ut, hi) + _dot(ut, lo)
        a_new = jnp.where(new_ok, jnp.exp2(pad_to_page(log_beta[n_past:]) + later), 0.0)
        a_ref[1, n_pages] = jnp.transpose(a_new).astype(BF16)
        carry = jnp.sum(rest_new, axis=0, keepdims=True)
        for pg in reversed(range(n_pages)):
            rows = slice(pg * page, (pg + 1) * page)
            hi, lo = _split_bf16(log_rest[rows])
            later = _dot(ut, hi) + _dot(ut, lo) + carry
            a_ref[1, pg] = jnp.transpose(jnp.exp2(log_beta[rows] + later)).astype(BF16)
            carry = carry + jnp.sum(log_rest[rows], axis=0, keepdims=True)

        a_new_d, a_new_s = a_ref[0, n_pages], a_ref[1, n_pages]
        vn_d, vn_s = pad_to_page(vnd_ref[...]), pad_to_page(vns_ref[...])
        for hp in range(n_pair):
            rows = slice(hp * pair, (hp + 1) * pair)
            cols = slice(hp * DIFF_V_DIM, (hp + 1) * DIFF_V_DIM)
            accd_ref[rows, :] = _dot(a_new_d[rows, :], vn_d[:, cols])
            both = _dot(a_new_s[rows, :], vn_s[:, cols])
            accs_ref[hp * pair:hp * pair + n_q, :] = both[:n_q, :SB_HEAD_DIM]
            accs_ref[hp * pair + n_q:(hp + 1) * pair, :] = both[n_q:, SB_HEAD_DIM:]

    attention_weights()

    def value_step(step, carry):
        prefetch(step + ahead, True)
        slot = slot_of(step, True)
        wait(value_copies(seq, step, slot))
        for p in range(pps):
            pg = step * pps + p
            for g, e_ref, m_ref, v_page, acc in ((0, ed_ref, md_ref, vdbuf.at[slot, p], accd_ref),
                                                 (1, es_ref, ms_ref, kbuf.at[slot, p], accs_ref)):
                spread = _dot(a_ref[g, pg], e_ref[...]).astype(BF16) * m_ref[...]
                acc[...] += _dot(spread, v_page[...].astype(BF16))
        return carry

    lax.fori_loop(0, n_steps, value_step, 0)

    for hp in range(n_pair):
        first = accd_ref[hp * pair:hp * pair + n_q, :]
        second = accd_ref[hp * pair + n_q:(hp + 1) * pair, :]
        o = first + second
        o = (o * lax.rsqrt(jnp.mean(o * o, axis=1, keepdims=True) + RMS_EPS)
             * subln_ref[...] * (1.0 - lam_init))
        od_ref[:, hp * DIFF_V_DIM:(hp + 1) * DIFF_V_DIM] = o.astype(od_ref.dtype)
    for h in range(n_groups):
        os_ref[:, h * SB_HEAD_DIM:(h + 1) * SB_HEAD_DIM] = accs_ref[h * n_q:(h + 1) * n_q, :].astype(os_ref.dtype)


def _sample_tail_bias(rel_bias, n_q, n_new, past_len):
    del past_len
    page = V7X_LANES
    n_rows = page + n_new
    near = _near_bias_log2(rel_bias)
    n_heads = near.shape[0]
    by_dist = jnp.concatenate([jnp.full((n_heads, n_new), NEG_INF, F32), near,
                               jnp.zeros((n_heads, n_q), F32)], axis=1)
    falling = by_dist[:, ::-1]
    per_query = [falling[:, n_q - 1 - i:n_q - 1 - i + n_rows] for i in range(n_q)]
    tail = jnp.stack(per_query, axis=-1)
    tail = jnp.broadcast_to(tail[:, :, None, :], (n_heads, n_rows, 2, n_q))
    return jnp.transpose(tail, (1, 0, 2, 3)).reshape(n_rows, n_heads * 2 * n_q)


def _sample_attention(qd, qs, knd, kns, vnd, vns, cache_kd, cache_ks, cache_vd, cache_vs, page_table,
                      lam_vecs, subln_w, rel_bias, *, lam_init, pages_per_step=4):
    bsz, n_q, w = qd.shape
    n_pages = page_table.shape[1]
    page = V7X_LANES
    n_groups = V7X_LANES // n_q
    n_heads_d = w // DIFF_V_DIM
    assert w == 8 * DIFF_V_DIM and n_q == 8
    assert cache_kd.shape[1:] == (page * n_groups, DIFF_HEAD_DIM) == cache_ks.shape[1:] == cache_vs.shape[1:]
    assert cache_vd.shape[1:] == (page * n_heads_d, DIFF_V_DIM)
    n_new = 2 * n_q
    pps = pages_per_step
    n_steps = n_pages // pps
    past_len = n_pages * page

    def pad_new(x):
        return jnp.pad(x, ((0, 0), (0, n_new - n_q), (0, 0)))

    def fixed(shape):
        return pl.BlockSpec(shape, lambda b, pt: (0,) * len(shape))

    def per_seq(rows):
        return pl.BlockSpec((None, rows, w), lambda b, pt: (b, 0, 0))

    in_hbm = pl.BlockSpec(memory_space=pl.ANY)

    def spread_and_keep(n_heads, rows_per_head):
        col = np.arange(page * n_heads)
        spread = (np.arange(page)[:, None] == col[None, :] // n_heads).astype(np.float32)
        keep = (np.arange(V7X_LANES)[:, None] // rows_per_head == col[None, :] % n_heads).astype(np.float32)
        return jnp.asarray(spread, dtype=BF16), jnp.asarray(keep, dtype=BF16)

    e_d, m_d = spread_and_keep(n_heads_d, 2 * n_q)
    e_s, m_s = spread_and_keep(n_groups, n_q)
    k_shape, vd_shape = (page * n_groups, DIFF_HEAD_DIM), (page * n_heads_d, DIFF_V_DIM)
    vec = fixed((1, DIFF_HEAD_DIM))
    in_specs = ([vec, vec, vec, vec, fixed((1, DIFF_V_DIM)), fixed((page + n_new, V7X_LANES)),
                 fixed((page, page)), fixed(e_d.shape), fixed(m_d.shape), fixed(e_s.shape), fixed(m_s.shape),
                 per_seq(n_q), per_seq(n_q),
                 per_seq(n_new), per_seq(n_new), per_seq(n_new), per_seq(n_new),
                 in_hbm, in_hbm, in_hbm, in_hbm])
    ut = jnp.asarray((np.arange(page)[None, :] > np.arange(page)[:, None]).astype(np.float32), dtype=BF16)
    args = [*lam_vecs, subln_w.reshape(1, DIFF_V_DIM).astype(F32),
            _sample_tail_bias(rel_bias, n_q, n_new, past_len), ut, e_d, m_d, e_s, m_s,
            qd, qs, pad_new(knd), pad_new(kns), pad_new(vnd), pad_new(vns),
            cache_kd, cache_ks, cache_vd, cache_vs]
    assert _PAGE_SLOTS - 1 <= n_steps and n_pages % pps == 0
    kern = functools.partial(_sample_attn_kernel, pages_per_step=pps, n_pages=n_pages, n_q=n_q,
                             n_new=n_new, lam_init=lam_init)
    n_keys = past_len + n_new
    grid_spec = pltpu.PrefetchScalarGridSpec(
        num_scalar_prefetch=1, grid=(bsz,), in_specs=in_specs,
        out_specs=[pl.BlockSpec((None, n_q, w), lambda b, pt: (b, 0, 0))] * 2,
        scratch_shapes=[pltpu.VMEM((2, V7X_LANES, w), BF16),
                        pltpu.VMEM((2, n_keys, V7X_LANES), F32),
                        pltpu.VMEM((2, n_pages + 1, V7X_LANES, page), BF16),
                        pltpu.VMEM((V7X_LANES, DIFF_V_DIM), F32),
                        pltpu.VMEM((V7X_LANES, SB_HEAD_DIM), F32),
                        pltpu.VMEM((_PAGE_SLOTS, 2 * pps) + k_shape, F32),
                        pltpu.VMEM((_PAGE_SLOTS, pps) + vd_shape, F32),
                        pltpu.SemaphoreType.DMA((_PAGE_SLOTS,))])
    return pl.pallas_call(
        kern, grid_spec=grid_spec,
        out_shape=[jax.ShapeDtypeStruct((bsz, n_q, w), F32)] * 2,
        compiler_params=_cparams(("arbitrary",)), name="sample_attn",
    )(page_table, *args)


def _cross_attn_prompt_kernel(q_ref, k_ref, v_ref, o_ref):
    s = _nt_dot(q_ref[...], k_ref[...])
    p = jnp.exp2(s - jnp.max(s, axis=1, keepdims=True))
    o = _dot(p.astype(BF16), v_ref[...]) / jnp.sum(p, axis=1, keepdims=True)
    o_ref[...] = o.astype(o_ref.dtype)


def _cross_attention_prompt(q, mk, mv, *, tq=1024):
    b, t, d = q.shape
    mem_len = mk.shape[1]
    hd = d // MEM_HEADS
    tq = _pick_tile(t, tq)
    q_spec = pl.BlockSpec((None, tq, hd), lambda bi, qi, hi: (bi, qi, hi))
    m_spec = pl.BlockSpec((None, mem_len, hd), lambda bi, qi, hi: (bi, 0, hi))
    return pl.pallas_call(
        _cross_attn_prompt_kernel, grid=(b, t // tq, MEM_HEADS), in_specs=[q_spec, m_spec, m_spec],
        out_specs=q_spec, out_shape=jax.ShapeDtypeStruct((b, t, d), BF16),
        compiler_params=_cparams(("parallel", "parallel", "parallel")), name="cross_attn_prompt",
    )(q, mk, mv)


def _cross_attn_sample_kernel(q_ref, k_ref, v_ref, o_ref, *, n_q):
    mem_len, n_heads, hd = k_ref.shape
    rows, lanes = mem_len * n_heads, V7X_LANES
    q = q_ref[...]
    q_by_head = jnp.concatenate([q[:, h * hd:(h + 1) * hd] for h in range(n_heads)]
                                + [jnp.zeros((lanes - n_heads * n_q, hd), F32)], axis=0)
    k = k_ref[...].reshape(rows, hd).astype(BF16)
    s = _nt_dot(k, q_by_head.astype(BF16))
    row_head = lax.broadcasted_iota(jnp.int32, (rows, lanes), 0) & (n_heads - 1)
    lane = lax.broadcasted_iota(jnp.int32, (rows, lanes), 1)
    s = jnp.where(row_head == (lane >> _log2(n_q)), s, NEG_INF)
    p = jnp.exp2(s - jnp.max(s, axis=0, keepdims=True))
    weights = jnp.transpose(p / jnp.sum(p, axis=0, keepdims=True)).astype(BF16)
    o = _dot(weights, v_ref[...].reshape(rows, hd).astype(BF16))
    for h in range(n_heads):
        o_ref[:, h * hd:(h + 1) * hd] = o[h * n_q:(h + 1) * n_q].astype(o_ref.dtype)


def _cross_attention_sample(q, mem_k, mem_v):
    b, n_q, d = q.shape
    mem_len = mem_k.shape[1]
    q_spec = pl.BlockSpec((None, n_q, d), lambda bi: (bi, 0, 0))
    m_spec = pl.BlockSpec((None, mem_len, MEM_HEADS, d // MEM_HEADS), lambda bi: (bi, 0, 0, 0))
    return pl.pallas_call(
        functools.partial(_cross_attn_sample_kernel, n_q=n_q), grid=(b,),
        in_specs=[q_spec, m_spec, m_spec], out_specs=q_spec,
        out_shape=jax.ShapeDtypeStruct((b, n_q, d), F32),
        compiler_params=_cparams(("parallel",)), name="cross_attn_sample",
    )(q, mem_k, mem_v)


def _weight_matmul(x, w, rounded, key, **kw):
    if rounded is None:
        return _matmul(x, w, emit_w=True, **kw)
    kw.pop("cols", None)
    return [*_matmul(x, rounded[key], **kw), None]


def _post_attention(x, mixed, cross_fn, wts, alpha, q_dtype, rounded):
    (w_out, ln1_g, ln1_b, w_cq, w_co, ln2_g, ln2_b, w_up, b_up, w_down, b_down, ln3_g, ln3_b) = wts
    d = x.shape[1]
    wb = {}
    y, wb["out"] = _weight_matmul(mixed, w_out, rounded, "out", out_dtypes=[F32], res=x, alpha=alpha,
                                  name="mm_out")
    x1, x1b = _layer_norm(y, ln1_g, ln1_b, name="ln1")
    qc, wb["cq"] = _weight_matmul(x1b, w_cq, rounded, "cq", out_dtypes=[q_dtype],
                                  out_scale=(d // MEM_HEADS) ** -0.5 * LOG2E, name="mm_cq")
    oc = cross_fn(qc).astype(BF16)
    y, wb["co"] = _weight_matmul(oc, w_co, rounded, "co", out_dtypes=[F32], res=x1, alpha=alpha, name="mm_co")
    x2, x2b = _layer_norm(y, ln2_g, ln2_b, name="ln2")
    hid, wb["up"] = _weight_matmul(x2b, w_up, rounded, "up", out_dtypes=[BF16], bias=b_up, act="relu2",
                                   name="mm_up")
    y, wb["down"] = _weight_matmul(hid, w_down, rounded, "down", out_dtypes=[F32], bias=b_down, res=x2,
                                   alpha=alpha, tk=2048, name="mm_down")
    x3, _ = _layer_norm(y, ln3_g, ln3_b, name="ln3")
    return x3, wb


_QKV_SECTIONS = ("qd", "kd", "vd", "qs", "ks", "vs")


def _project_qkv(xb, w_in, q_scale, q_dtype, rounded):
    dw = w_in.shape[1] // len(_QKV_SECTIONS)
    res, wb = {}, {}
    for i, sec in enumerate(_QKV_SECTIONS):
        kw = (dict(out_dtypes=[q_dtype], out_scale=q_scale) if sec[0] == "q" else dict(out_dtypes=[F32, BF16]))
        *res[sec], wb[sec] = _weight_matmul(xb, w_in, rounded, sec, cols=(i * dw, dw), name="mm_" + sec, **kw)
    return res, wb


def kernel(x_prompt, x_sample, mem_prompt, cache_diff_k, cache_diff_v, cache_sb_k, cache_sb_v, cache_mem_k, cache_mem_v, page_table, w_in, lambda_q1, lambda_k1, lambda_q2, lambda_k2, subln_w, rel_bias, w_out, ln1_g, ln1_b, w_cq, w_ck, w_cv, w_co, ln2_g, ln2_b, w_up, b_up, w_down, b_down, ln3_g, ln3_b):
    depth = w_in.shape[0]
    assert depth == 1, "single-layer step"
    bsz, seq, d = x_prompt.shape
    dec_b, dec_t, _ = x_sample.shape
    mem_len = mem_prompt.shape[1]
    n_pool, page = cache_diff_k.shape[1], cache_diff_k.shape[2]
    dw = d // 2
    alpha = (2.0 * depth) ** 0.25
    l = 0
    lam_init = 0.8 - 0.6 * math.exp(-0.3 * l)
    q_scale = DIFF_HEAD_DIM ** -0.5 * LOG2E
    assert DIFF_HEAD_DIM == SB_HEAD_DIM

    tail_w = (w_out[l], ln1_g[l], ln1_b[l], w_cq[l], w_co[l], ln2_g[l], ln2_b[l], w_up[l], b_up[l],
              w_down[l], b_down[l], ln3_g[l], ln3_b[l])
    lam_vecs = [v[l].reshape(1, DIFF_HEAD_DIM).astype(F32) for v in (lambda_q1, lambda_k1, lambda_q2, lambda_k2)]
    vnear = _near_bias_log2(rel_bias)
    vtab = jnp.concatenate([vnear[:, :1], vnear[:, :0:-1]], axis=1)

    xs = x_sample.reshape(dec_b * dec_t, d)
    sp, w_in_b = _project_qkv(xs.astype(BF16), w_in[l], q_scale, F32, None)
    s3 = (dec_b, dec_t, dw)
    smix_d, smix_s = _sample_attention(
        sp["qd"][0].reshape(s3), sp["qs"][0].reshape(s3), sp["kd"][1].reshape(s3), sp["ks"][1].reshape(s3),
        sp["vd"][1].reshape(s3), sp["vs"][1].reshape(s3),
        cache_diff_k[l].reshape(n_pool, -1, DIFF_HEAD_DIM), cache_sb_k[l].reshape(n_pool, -1, SB_HEAD_DIM),
        cache_diff_v[l].reshape(n_pool, -1, DIFF_V_DIM), cache_sb_v[l].reshape(n_pool, -1, SB_HEAD_DIM),
        page_table, lam_vecs, subln_w[l], rel_bias, lam_init=lam_init)
    smixed = jnp.concatenate([smix_d, smix_s], axis=-1).reshape(dec_b * dec_t, d).astype(BF16)

    def cross_sample(qc):
        o = _cross_attention_sample(qc.reshape(dec_b, dec_t, d), cache_mem_k[l], cache_mem_v[l])
        return o.reshape(dec_b * dec_t, d)

    y_sample, tail_b = _post_attention(xs, smixed, cross_sample, tail_w, alpha, F32, None)
    y_sample = y_sample.reshape(dec_b, dec_t, d)
    skd32, svd32, sks32, svs32 = (sp[sec][0] for sec in ("kd", "vd", "ks", "vs"))

    xp = x_prompt.reshape(bsz * seq, d)
    pp, _ = _project_qkv(xp.astype(BF16), w_in[l], q_scale, BF16, w_in_b)
    shape3 = (bsz, seq, dw)
    mixed_d = _diff_attention_prompt(pp["qd"][0].reshape(shape3), pp["kd"][1].reshape(shape3),
                                     pp["vd"][1].reshape(shape3), lam_vecs, vtab, subln_w[l], lam_init=lam_init)
    mixed_s = _sb_attention_prompt(pp["qs"][0].reshape(shape3), pp["ks"][1].reshape(shape3),
                                   pp["vs"][1].reshape(shape3))
    mixed = jnp.concatenate([mixed_d, mixed_s], axis=-1).reshape(bsz * seq, d)
    memb = mem_prompt.reshape(bsz * mem_len, d).astype(BF16)
    mk32, mk = _matmul(memb, w_ck[l], out_dtypes=[F32, BF16], name="mm_ck")
    mv32, mv = _matmul(memb, w_cv[l], out_dtypes=[F32, BF16], name="mm_cv")

    def cross_prompt(qc):
        o = _cross_attention_prompt(qc.reshape(bsz, seq, d), mk.reshape(bsz, mem_len, d),
                                    mv.reshape(bsz, mem_len, d))
        return o.reshape(bsz * seq, d)

    y_prompt, _ = _post_attention(xp, mixed, cross_prompt, tail_w, alpha, BF16, tail_b)
    y_prompt = y_prompt.reshape(bsz, seq, d)
    kd32, vd32, ks32, vs32 = (pp[sec][0] for sec in ("kd", "vd", "ks", "vs"))

    dh, hd = DIFF_HEAD_DIM, dw // DIFF_V_DIM
    hs = dw // SB_HEAD_DIM
    mh = d // MEM_HEADS
    return (y_prompt, y_sample,
            kd32.reshape(1, bsz, seq, hd, 2, dh), vd32.reshape(1, bsz, seq, hd, 2 * dh),
            ks32.reshape(1, bsz, seq, hs, SB_HEAD_DIM), vs32.reshape(1, bsz, seq, hs, SB_HEAD_DIM),
            mk32.reshape(1, bsz, mem_len, MEM_HEADS, mh), mv32.reshape(1, bsz, mem_len, MEM_HEADS, mh),
            skd32.reshape(1, dec_b, dec_t, hd, 2, dh), svd32.reshape(1, dec_b, dec_t, hd, 2 * dh),
            sks32.reshape(1, dec_b, dec_t, hs, SB_HEAD_DIM), svs32.reshape(1, dec_b, dec_t, hs, SB_HEAD_DIM))
```

```python
import functools
import math

import numpy as np
import jax
import jax.numpy as jnp
from jax import lax
from jax.experimental import pallas as pl
from jax.experimental.pallas import tpu as pltpu

F32 = jnp.float32
BF16 = jnp.bfloat16

DIFF_HEAD_DIM = 128
DIFF_V_DIM = 2 * DIFF_HEAD_DIM
SB_HEAD_DIM = 128
MEM_HEADS = 4
N_BUCKETS = 32
MAX_DISTANCE = 128
LN_EPS = 1e-5
RMS_EPS = 1e-5
NEG_INF = -1e30
LOG2E = math.log2(math.e)

V7X_LANES = 128
V7X_VMEM_BYTES = 64 * 1024 * 1024
VMEM_LIMIT_BYTES = V7X_VMEM_BYTES - 8 * 1024 * 1024


def _cparams(semantics):
    return pltpu.CompilerParams(dimension_semantics=semantics, vmem_limit_bytes=VMEM_LIMIT_BYTES)


def _nt_dot(a, b):
    return lax.dot_general(a, b, (((1,), (1,)), ((), ())), preferred_element_type=F32)


def _dot(a, b):
    return jnp.dot(a, b, preferred_element_type=F32)


def _tile_lanes(x, n):
    return x if n == 1 else jnp.concatenate([x] * n, axis=1)


def _mm_kernel(*refs, nk, has_bias, has_res, act, alpha, out_scale, n_out, emit_w):
    x_ref, w_ref = refs[0], refs[1]
    pos = 2
    bias_ref = res_ref = w_out_ref = None
    if has_bias:
        bias_ref = refs[pos]
        pos += 1
    if has_res:
        res_ref = refs[pos]
        pos += 1
    out_refs = refs[pos:pos + n_out]
    pos += n_out
    if emit_w:
        w_out_ref = refs[pos]
        pos += 1
    acc_ref = refs[pos] if nk > 1 else None

    def epilogue(y):
        if has_bias:
            y = y + bias_ref[...]
        if act == "relu2":
            y = jnp.square(jnp.maximum(y, 0.0))
        if has_res:
            y = alpha * res_ref[...] + y
        if out_scale != 1.0:
            y = y * out_scale
        for o_ref in out_refs:
            o_ref[...] = y.astype(o_ref.dtype)

    w = w_ref[...].astype(BF16)
    if emit_w:
        w_out_ref[...] = w
    part = _dot(x_ref[...], w)
    if nk == 1:
        epilogue(part)
    else:
        k = pl.program_id(2)

        @pl.when(k == 0)
        def _():
            acc_ref[...] = part

        @pl.when(jnp.logical_and(k > 0, k < nk - 1))
        def _():
            acc_ref[...] += part

        @pl.when(k == nk - 1)
        def _():
            epilogue(acc_ref[...] + part)


_MAX_RESIDENT_K = 4096


def _pick_tile(n, pref):
    t = min(n, pref)
    while n % t:
        t //= 2
    return t


def _matmul(x, w, *, out_dtypes, bias=None, res=None, act=None, alpha=1.0, out_scale=1.0,
            tm=1024, tn=1024, tk=4096, cols=None, emit_w=False, name="mm"):
    m, kdim = x.shape
    col0, n = cols if cols is not None else (0, w.shape[1])
    if w.dtype == F32:
        tm = m
        tn, tk = (256, kdim) if kdim <= _MAX_RESIDENT_K else (1024, 1024)
    tm, tn, tk = _pick_tile(m, tm), _pick_tile(n, tn), _pick_tile(kdim, tk)
    assert col0 % tn == 0 and (w.dtype == BF16 or m == tm) and (w.dtype == F32 or not emit_w)
    j0 = col0 // tn
    nk = kdim // tk
    in_specs = [pl.BlockSpec((tm, tk), lambda i, j, k: (i, k)),
                pl.BlockSpec((tk, tn), lambda i, j, k: (k, j + j0))]
    args = [x, w]
    if bias is not None:
        in_specs.append(pl.BlockSpec((1, tn), lambda i, j, k: (0, j)))
        args.append(bias.reshape(1, n).astype(F32))
    if res is not None:
        in_specs.append(pl.BlockSpec((tm, tn), lambda i, j, k: (i, j)))
        args.append(res)
    out_specs = [pl.BlockSpec((tm, tn), lambda i, j, k: (i, j)) for _ in out_dtypes]
    out_shape = [jax.ShapeDtypeStruct((m, n), dt) for dt in out_dtypes]
    if emit_w:
        out_specs.append(pl.BlockSpec((tk, tn), lambda i, j, k: (k, j)))
        out_shape.append(jax.ShapeDtypeStruct((kdim, n), BF16))
    scratch = [pltpu.VMEM((tm, tn), F32)] if nk > 1 else []
    kern = functools.partial(_mm_kernel, nk=nk, has_bias=bias is not None, has_res=res is not None,
                             act=act, alpha=alpha, out_scale=out_scale, n_out=len(out_dtypes), emit_w=emit_w)
    return pl.pallas_call(
        kern, grid=(m // tm, n // tn, nk), in_specs=in_specs, out_specs=out_specs,
        out_shape=out_shape, scratch_shapes=scratch,
        compiler_params=_cparams(("parallel", "parallel", "arbitrary")), name=name,
    )(*args)


def _ln_kernel(y_ref, g_ref, b_ref, *o_refs):
    y = y_ref[...]
    mu = jnp.mean(y, axis=-1, keepdims=True)
    d = y - mu
    var = jnp.mean(jnp.square(d), axis=-1, keepdims=True)
    o = d * lax.rsqrt(var + LN_EPS) * g_ref[...] + b_ref[...]
    for o_ref in o_refs:
        o_ref[...] = o.astype(o_ref.dtype)


def _layer_norm(y, g, b, *, out_dtypes=(F32, BF16), tm=256, name="ln"):
    m, d = y.shape
    tm = _pick_tile(m, tm)
    row = pl.BlockSpec((tm, d), lambda i: (i, 0))
    vec = pl.BlockSpec((1, d), lambda i: (0, 0))
    return pl.pallas_call(
        _ln_kernel, grid=(m // tm,), in_specs=[row, vec, vec], out_specs=[row] * len(out_dtypes),
        out_shape=[jax.ShapeDtypeStruct((m, d), dt) for dt in out_dtypes],
        compiler_params=_cparams(("parallel",)), name=name,
    )(y, g.reshape(1, d).astype(F32), b.reshape(1, d).astype(F32))


def _bucket_table(n):
    max_exact = N_BUCKETS // 2
    d = np.arange(n)
    df = np.maximum(d, 1).astype(np.float32)
    ratio = np.log(df / np.float32(max_exact)) / np.float32(math.log(MAX_DISTANCE / max_exact))
    large = max_exact + (ratio * np.float32(N_BUCKETS - max_exact)).astype(np.int32)
    large = np.minimum(large, N_BUCKETS - 1)
    return np.where(d < max_exact, d, large).astype(np.int32)


def _near_bias_log2(rel_bias):
    tab = rel_bias.astype(F32)[_bucket_table(MAX_DISTANCE)]
    far = rel_bias.astype(F32)[N_BUCKETS - 1]
    return jnp.transpose(tab - far[None, :]) * LOG2E


def _lambda_value(lq1_ref, lk1_ref, lq2_ref, lk2_ref, lam_init):
    s1 = jnp.sum(lq1_ref[...] * lk1_ref[...], axis=1, keepdims=True)
    s2 = jnp.sum(lq2_ref[...] * lk2_ref[...], axis=1, keepdims=True)
    return jnp.exp(s1) - jnp.exp(s2) + lam_init


def _diff_attn_kernel(lq1_ref, lk1_ref, lq2_ref, lk2_ref, vtab_ref, subln_ref, q_ref, k_ref, v_ref,
                      o_ref, bdiag_ref, bsub_ref, m_ref, l_ref, acc_ref, *, tq, tk, lam_init):
    qi = pl.program_id(2)
    dh = DIFF_HEAD_DIM
    nb = tq // V7X_LANES

    @pl.when(qi == 0)
    def _build_bias_tiles():
        toep = pltpu.roll(jnp.broadcast_to(vtab_ref[...], (V7X_LANES, V7X_LANES)), 0, 1,
                          stride=1, stride_axis=0)
        ii = lax.broadcasted_iota(jnp.int32, (V7X_LANES, V7X_LANES), 0)
        jj = lax.broadcasted_iota(jnp.int32, (V7X_LANES, V7X_LANES), 1)
        on_diag = jnp.where(ii >= jj, toep, NEG_INF)
        below = jnp.where(jj > ii, toep, 0.0)
        zeros = jnp.zeros((V7X_LANES, V7X_LANES), F32)
        masked = jnp.full((V7X_LANES, V7X_LANES), NEG_INF, F32)
        for rb in range(nb):
            for cb in range(nb):
                rows = slice(rb * V7X_LANES, (rb + 1) * V7X_LANES)
                cols = slice(cb * V7X_LANES, (cb + 1) * V7X_LANES)
                if rb == cb:
                    tile = on_diag
                elif rb == cb + 1:
                    tile = below
                elif rb > cb:
                    tile = zeros
                else:
                    tile = masked
                bdiag_ref[rows, cols] = tile
        nbk = tk // V7X_LANES
        for rb in range(nb):
            for cb in range(nbk):
                rows = slice(rb * V7X_LANES, (rb + 1) * V7X_LANES)
                cols = slice(cb * V7X_LANES, (cb + 1) * V7X_LANES)
                bsub_ref[rows, cols] = below if (rb == 0 and cb == nbk - 1) else zeros

    q = q_ref[...]

    def scores(c, kc):
        return _nt_dot(q[:, c * dh:(c + 1) * dh], kc[:, c * dh:(c + 1) * dh])

    n_sub = tq // tk
    n_rep_s, n_rep_v = tk // V7X_LANES, DIFF_V_DIM // V7X_LANES

    def kv_chunk(block, r):
        start = pl.multiple_of(block * tq + r * tk, tk)
        return k_ref[pl.ds(start, tk), :], v_ref[pl.ds(start, tk), :]

    def update(c, s, vc, row0=0):
        rows = slice(row0, tq)
        m_prev = m_ref[c, rows, :]
        m_new = jnp.maximum(m_prev, jnp.max(s, axis=1, keepdims=True))
        scale = jnp.exp2(m_prev - m_new)
        p = jnp.exp2(s - _tile_lanes(m_new, n_rep_s))
        l_ref[c, rows, :] = scale * l_ref[c, rows, :] + jnp.sum(p, axis=1, keepdims=True)
        acc_ref[c, rows, :] = _tile_lanes(scale, n_rep_v) * acc_ref[c, rows, :] + _dot(p.astype(BF16), vc)
        m_ref[c, rows, :] = m_new

    for r in range(n_sub):
        kc, vc = kv_chunk(qi, r)
        for c in range(2):
            if r > 0:
                s = _nt_dot(q[r * tk:, c * dh:(c + 1) * dh], kc[:, c * dh:(c + 1) * dh])
                update(c, s + bdiag_ref[r * tk:, r * tk:(r + 1) * tk], vc, row0=r * tk)
                continue
            s = scores(c, kc) + bdiag_ref[:, :tk]
            m = jnp.broadcast_to(jnp.max(s, axis=1, keepdims=True), (tq, V7X_LANES))
            p = jnp.exp2(s - _tile_lanes(m, n_rep_s))
            m_ref[c] = m
            l_ref[c] = jnp.broadcast_to(jnp.sum(p, axis=1, keepdims=True), (tq, V7X_LANES))
            acc_ref[c] = _dot(p.astype(BF16), vc)

    @pl.when(qi >= 1)
    def _sub_diagonal():
        for r in range(n_sub):
            kc1, vc1 = kv_chunk(qi - 1, r)
            for c in range(2):
                s = scores(c, kc1)
                if r == n_sub - 1:
                    s = s + bsub_ref[...]
                update(c, s, vc1)

    def far_body(j, carry):
        for r in range(n_sub):
            kcj, vcj = kv_chunk(j, r)
            for c in range(2):
                update(c, scores(c, kcj), vcj)
        return carry

    lax.fori_loop(0, jnp.maximum(qi - 1, 0), far_body, 0)

    lam = _lambda_value(lq1_ref, lk1_ref, lq2_ref, lk2_ref, lam_init)
    w1 = _tile_lanes(1.0 / l_ref[0], n_rep_v)
    w2 = _tile_lanes(lam / l_ref[1], n_rep_v)
    o = acc_ref[0] * w1 - acc_ref[1] * w2
    o = o * lax.rsqrt(jnp.mean(o * o, axis=1, keepdims=True) + RMS_EPS) * subln_ref[...] * (1.0 - lam_init)
    o_ref[...] = o.astype(o_ref.dtype)


def _diff_attention_prompt(qd, kd, vd, lam_vecs, vtab, subln_w, *, lam_init, tq=1024, tk=256):
    b, t, w = qd.shape
    h = w // DIFF_V_DIM
    tq = _pick_tile(t, tq)
    tk = _pick_tile(tq, tk)
    vec = pl.BlockSpec((1, DIFF_HEAD_DIM), lambda bi, hi, qi: (0, 0))
    in_specs = [vec, vec, vec, vec,
                pl.BlockSpec((None, 1, V7X_LANES), lambda bi, hi, qi: (hi, 0, 0)),
                pl.BlockSpec((1, DIFF_V_DIM), lambda bi, hi, qi: (0, 0)),
                pl.BlockSpec((None, tq, DIFF_V_DIM), lambda bi, hi, qi: (bi, qi, hi)),
                pl.BlockSpec((None, t, DIFF_V_DIM), lambda bi, hi, qi: (bi, 0, hi)),
                pl.BlockSpec((None, t, DIFF_V_DIM), lambda bi, hi, qi: (bi, 0, hi))]
    kern = functools.partial(_diff_attn_kernel, tq=tq, tk=tk, lam_init=lam_init)
    return pl.pallas_call(
        kern, grid=(b, h, t // tq), in_specs=in_specs,
        out_specs=pl.BlockSpec((None, tq, DIFF_V_DIM), lambda bi, hi, qi: (bi, qi, hi)),
        out_shape=jax.ShapeDtypeStruct((b, t, w), BF16),
        scratch_shapes=[pltpu.VMEM((tq, tq), F32), pltpu.VMEM((tq, tk), F32),
                        pltpu.VMEM((2, tq, V7X_LANES), F32), pltpu.VMEM((2, tq, V7X_LANES), F32),
                        pltpu.VMEM((2, tq, DIFF_V_DIM), F32)],
        compiler_params=_cparams(("arbitrary", "arbitrary", "arbitrary")), name="diff_attn_prompt",
    )(*lam_vecs, vtab.reshape(h, 1, V7X_LANES), subln_w.reshape(1, DIFF_V_DIM).astype(F32), qd, kd, vd)


def _later_matrix(n):
    u = (np.arange(n)[:, None] > np.arange(n)[None, :]).astype(np.float32)
    return jnp.asarray(np.concatenate([u, u], axis=0), dtype=BF16)


def _stick_break_terms(z):
    log_beta = jnp.minimum(z, 0.0) - jnp.log2(1.0 + jnp.exp2(-jnp.abs(z)))
    return log_beta, log_beta - z


def _split_bf16(x):
    hi = x.astype(BF16)
    lo = (x - hi.astype(F32)).astype(BF16)
    return hi, lo


def _sb_attn_kernel(u2_ref, q_ref, k_ref, v_ref, o_ref, c_ref, acc_ref, *, tq, tk):
    qi = pl.program_id(2)
    n_sub = tq // tk
    q = q_ref[...]
    c_ref[...] = jnp.zeros_like(c_ref)
    acc_ref[...] = jnp.zeros_like(acc_ref)

    def process(chunks, row0=0):
        rows = slice(row0, tq)
        c = c_ref[rows, :]
        weights, values = [], []
        for start, diag_offset in chunks:
            kc = k_ref[pl.ds(start, tk), :]
            values.append(v_ref[pl.ds(start, tk), :])
            log_beta, log_rest = _stick_break_terms(_nt_dot(q[row0:], kc))
            if diag_offset is not None:
                ii = lax.broadcasted_iota(jnp.int32, (tq - row0, tk), 0) + row0
                jj = lax.broadcasted_iota(jnp.int32, (tq - row0, tk), 1)
                mask = (jj + diag_offset) < ii
                log_rest = jnp.where(mask, log_rest, 0.0)
            hi, lo = _split_bf16(log_rest)
            later = _dot(jnp.concatenate([hi, lo], axis=1), u2_ref[...]) + _tile_lanes(c, tk // V7X_LANES)
            a = jnp.exp2(log_beta + later)
            if diag_offset is not None:
                a = jnp.where(mask, a, 0.0)
            weights.append(a.astype(BF16))
            c = c + jnp.sum(log_rest, axis=1, keepdims=True)
        c_ref[rows, :] = c
        acc_ref[rows, :] += _dot(jnp.concatenate(weights, axis=1), jnp.concatenate(values, axis=0))

    for r in reversed(range(n_sub)):
        process([(pl.multiple_of(qi * tq + r * tk, tk), r * tk)], row0=r * tk)

    n_far = qi

    def far_body(j, carry):
        base = (n_far - 1 - j) * tq
        process([(pl.multiple_of(base + r * tk, tk), None) for r in reversed(range(n_sub))])
        return carry

    lax.fori_loop(0, n_far, far_body, 0)
    o_ref[...] = acc_ref[...].astype(o_ref.dtype)


def _sb_attention_prompt(qs, ks, vs, *, tq=1024, tk=256):
    b, t, w = qs.shape
    h = w // SB_HEAD_DIM
    tq = _pick_tile(t, tq)
    tk = _pick_tile(tq, tk)
    in_specs = [pl.BlockSpec((2 * tk, tk), lambda bi, hi, qi: (0, 0)),
                pl.BlockSpec((None, tq, SB_HEAD_DIM), lambda bi, hi, qi: (bi, qi, hi)),
                pl.BlockSpec((None, t, SB_HEAD_DIM), lambda bi, hi, qi: (bi, 0, hi)),
                pl.BlockSpec((None, t, SB_HEAD_DIM), lambda bi, hi, qi: (bi, 0, hi))]
    kern = functools.partial(_sb_attn_kernel, tq=tq, tk=tk)
    return pl.pallas_call(
        kern, grid=(b, h, t // tq), in_specs=in_specs,
        out_specs=pl.BlockSpec((None, tq, SB_HEAD_DIM), lambda bi, hi, qi: (bi, qi, hi)),
        out_shape=jax.ShapeDtypeStruct((b, t, w), BF16),
        scratch_shapes=[pltpu.VMEM((tq, V7X_LANES), F32), pltpu.VMEM((tq, SB_HEAD_DIM), F32)],
        compiler_params=_cparams(("parallel", "parallel", "parallel")), name="sb_attn_prompt",
    )(_later_matrix(tk), qs, ks, vs)


def _expand_queries(q, n_groups, group_cols):
    r, c = q.shape
    tiled = jnp.tile(q, (n_groups, 1))
    row_g = lax.broadcasted_iota(jnp.int32, (n_groups * r, c), 0) >> _log2(r)
    col_g = lax.broadcasted_iota(jnp.int32, (n_groups * r, c), 1) >> _log2(group_cols)
    return jnp.where(row_g == col_g, tiled, 0.0).astype(BF16)


def _log2(n):
    assert n & (n - 1) == 0
    return n.bit_length() - 1


_PAGE_SLOTS = 3


def _sample_attn_kernel(pt_ref, lq1_ref, lk1_ref, lq2_ref, lk2_ref, subln_ref, btail_ref, ut_ref,
                        ed_ref, md_ref, es_ref, ms_ref,
                        qd_ref, qs_ref, knd_ref, kns_ref, vnd_ref, vns_ref, *rest,
                        pages_per_step, n_pages, n_q, n_new, lam_init):
    pps = pages_per_step
    (kd_hbm, ks_hbm, vd_hbm, vs_hbm, od_ref, os_ref,
     qexp_ref, s_ref, a_ref, accd_ref, accs_ref, kbuf, vdbuf, sem) = rest
    seq, n_seq = pl.program_id(0), pl.num_programs(0)
    n_steps = n_pages // pps
    ahead = _PAGE_SLOTS - 1

    def slot_of(key_or_value_step, is_value):
        return lax.rem(seq * (2 * n_steps) + key_or_value_step + (n_steps if is_value else 0), _PAGE_SLOTS)

    def key_copies(which_seq, step, slot):
        out = []
        for p in range(pps):
            pg = pt_ref[which_seq, step * pps + p]
            out.append(pltpu.make_async_copy(kd_hbm.at[pg], kbuf.at[slot, p], sem.at[slot]))
            out.append(pltpu.make_async_copy(ks_hbm.at[pg], kbuf.at[slot, pps + p], sem.at[slot]))
        return out

    def value_copies(which_seq, step, slot):
        out = []
        for p in range(pps):
            pg = pt_ref[which_seq, step * pps + p]
            out.append(pltpu.make_async_copy(vd_hbm.at[pg], vdbuf.at[slot, p], sem.at[slot]))
            out.append(pltpu.make_async_copy(vs_hbm.at[pg], kbuf.at[slot, p], sem.at[slot]))
        return out

    def start(copies):
        for c in copies:
            c.start()

    def wait(copies):
        for c in copies:
            c.wait()

    def prefetch(step_after, is_value):
        nxt = step_after
        slot = slot_of(nxt, is_value)
        if not is_value:
            @pl.when(nxt < n_steps)
            def _():
                start(key_copies(seq, nxt, slot))

            @pl.when(nxt >= n_steps)
            def _():
                start(value_copies(seq, nxt - n_steps, slot))
        else:
            @pl.when(nxt < n_steps)
            def _():
                start(value_copies(seq, nxt, slot))

            @pl.when(jnp.logical_and(nxt >= n_steps, seq + 1 < n_seq))
            def _():
                start(key_copies(seq + 1, nxt - n_steps, slot))

    @pl.when(seq == 0)
    def _first_pages():
        for step in range(ahead):
            start(key_copies(0, step, slot_of(step, False)))

    page = V7X_LANES
    n_past = n_pages * page
    lanes = V7X_LANES
    pair = 2 * n_q
    n_pair = lanes // pair
    n_groups = lanes // n_q

    def pad_to_page(x):
        return jnp.concatenate([x, jnp.zeros((page - n_new, x.shape[1]), x.dtype)], axis=0)

    def keys_by_group(k_ref):
        slabs = [k_ref[pl.ds(g, page, stride=n_groups), :] for g in range(n_groups)]
        return jnp.concatenate(slabs, axis=1).astype(BF16)

    qexp_ref[0] = _expand_queries(qd_ref[...], n_groups, DIFF_HEAD_DIM)
    qexp_ref[1] = _expand_queries(qs_ref[...], n_groups, SB_HEAD_DIM)
    s_ref[0, n_past:, :] = _nt_dot(knd_ref[...], qexp_ref[0])
    s_ref[1, n_past:, :] = _nt_dot(kns_ref[...], qexp_ref[1])

    def key_step(step, carry):
        prefetch(step + ahead, False)
        slot = slot_of(step, False)
        wait(key_copies(seq, step, slot))
        for p in range(pps):
            row = pl.multiple_of((step * pps + p) * page, page)
            s_ref[0, pl.ds(row, page), :] = _nt_dot(keys_by_group(kbuf.at[slot, p]), qexp_ref[0])
            s_ref[1, pl.ds(row, page), :] = _nt_dot(keys_by_group(kbuf.at[slot, pps + p]), qexp_ref[1])
        return carry

    lax.fori_loop(0, n_steps, key_step, 0)

    def attention_weights():
        lane = lax.broadcasted_iota(jnp.int32, (1, lanes), 1)
        lam = _lambda_value(lq1_ref, lk1_ref, lq2_ref, lk2_ref, lam_init)
        s_ref[0, n_past - page:, :] = s_ref[0, n_past - page:, :] + btail_ref[...]
        s = s_ref[0]
        m = jnp.max(s, axis=0, keepdims=True)
        p = jnp.exp2(s - m)
        is_first = ((lane >> _log2(n_q)) & 1) == 0
        coef = jnp.where(is_first, 1.0, -lam) / jnp.sum(p, axis=0, keepdims=True)
        for pg in range(n_pages):
            a_ref[0, pg] = jnp.transpose(p[pg * page:(pg + 1) * page] * coef).astype(BF16)
        a_ref[0, n_pages] = jnp.transpose(pad_to_page(p[n_past:] * coef)).astype(BF16)

        log_beta, log_rest = _stick_break_terms(s_ref[1])
        row = lax.broadcasted_iota(jnp.int32, (page, lanes), 0)
        new_ok = row < (lax.broadcasted_iota(jnp.int32, (page, lanes), 1) & (n_q - 1))
        ut = ut_ref[...]
        rest_new = jnp.where(new_ok, pad_to_page(log_rest[n_past:]), 0.0)
        hi, lo = _split_bf16(rest_new)
        later = _dot(ut, hi) + _dot(ut, lo)
        a_new = jnp.where(new_ok, jnp.exp2(pad_to_page(log_beta[n_past:]) + later), 0.0)
        a_ref[1, n_pages] = jnp.transpose(a_new).astype(BF16)
        carry = jnp.sum(rest_new, axis=0, keepdims=True)
        for pg in reversed(range(n_pages)):
            rows = slice(pg * page, (pg + 1) * page)
            hi, lo = _split_bf16(log_rest[rows])
            later = _dot(ut, hi) + _dot(ut, lo) + carry
            a_ref[1, pg] = jnp.transpose(jnp.exp2(log_beta[rows] + later)).astype(BF16)
            carry = carry + jnp.sum(log_rest[rows], axis=0, keepdims=True)

        a_new_d, a_new_s = a_ref[0, n_pages], a_ref[1, n_pages]
        vn_d, vn_s = pad_to_page(vnd_ref[...]), pad_to_page(vns_ref[...])
        for hp in range(n_pair):
            rows = slice(hp * pair, (hp + 1) * pair)
            cols = slice(hp * DIFF_V_DIM, (hp + 1) * DIFF_V_DIM)
            accd_ref[rows, :] = _dot(a_new_d[rows, :], vn_d[:, cols])
            both = _dot(a_new_s[rows, :], vn_s[:, cols])
            accs_ref[hp * pair:hp * pair + n_q, :] = both[:n_q, :SB_HEAD_DIM]
            accs_ref[hp * pair + n_q:(hp + 1) * pair, :] = both[n_q:, SB_HEAD_DIM:]

    attention_weights()

    def value_step(step, carry):
        prefetch(step + ahead, True)
        slot = slot_of(step, True)
        wait(value_copies(seq, step, slot))
        for p in range(pps):
            pg = step * pps + p
            for g, e_ref, m_ref, v_page, acc in ((0, ed_ref, md_ref, vdbuf.at[slot, p], accd_ref),
                                                 (1, es_ref, ms_ref, kbuf.at[slot, p], accs_ref)):
                spread = _dot(a_ref[g, pg], e_ref[...]).astype(BF16) * m_ref[...]
                acc[...] += _dot(spread, v_page[...].astype(BF16))
        return carry

    lax.fori_loop(0, n_steps, value_step, 0)

    for hp in range(n_pair):
        first = accd_ref[hp * pair:hp * pair + n_q, :]
        second = accd_ref[hp * pair + n_q:(hp + 1) * pair, :]
        o = first + second
        o = (o * lax.rsqrt(jnp.mean(o * o, axis=1, keepdims=True) + RMS_EPS)
             * subln_ref[...] * (1.0 - lam_init))
        od_ref[:, hp * DIFF_V_DIM:(hp + 1) * DIFF_V_DIM] = o.astype(od_ref.dtype)
    for h in range(n_groups):
        os_ref[:, h * SB_HEAD_DIM:(h + 1) * SB_HEAD_DIM] = accs_ref[h * n_q:(h + 1) * n_q, :].astype(os_ref.dtype)


def _sample_tail_bias(rel_bias, n_q, n_new, past_len):
    del past_len
    page = V7X_LANES
    n_rows = page + n_new
    near = _near_bias_log2(rel_bias)
    n_heads = near.shape[0]
    by_dist = jnp.concatenate([jnp.full((n_heads, n_new), NEG_INF, F32), near,
                               jnp.zeros((n_heads, n_q), F32)], axis=1)
    falling = by_dist[:, ::-1]
    per_query = [falling[:, n_q - 1 - i:n_q - 1 - i + n_rows] for i in range(n_q)]
    tail = jnp.stack(per_query, axis=-1)
    tail = jnp.broadcast_to(tail[:, :, None, :], (n_heads, n_rows, 2, n_q))
    return jnp.transpose(tail, (1, 0, 2, 3)).reshape(n_rows, n_heads * 2 * n_q)


def _sample_attention(qd, qs, knd, kns, vnd, vns, cache_kd, cache_ks, cache_vd, cache_vs, page_table,
                      lam_vecs, subln_w, rel_bias, *, lam_init, pages_per_step=4):
    bsz, n_q, w = qd.shape
    n_pages = page_table.shape[1]
    page = V7X_LANES
    n_groups = V7X_LANES // n_q
    n_heads_d = w // DIFF_V_DIM
    assert w == 8 * DIFF_V_DIM and n_q == 8
    assert cache_kd.shape[1:] == (page * n_groups, DIFF_HEAD_DIM) == cache_ks.shape[1:] == cache_vs.shape[1:]
    assert cache_vd.shape[1:] == (page * n_heads_d, DIFF_V_DIM)
    n_new = 2 * n_q
    pps = pages_per_step
    n_steps = n_pages // pps
    past_len = n_pages * page

    def pad_new(x):
        return jnp.pad(x, ((0, 0), (0, n_new - n_q), (0, 0)))

    def fixed(shape):
        return pl.BlockSpec(shape, lambda b, pt: (0,) * len(shape))

    def per_seq(rows):
        return pl.BlockSpec((None, rows, w), lambda b, pt: (b, 0, 0))

    in_hbm = pl.BlockSpec(memory_space=pl.ANY)

    def spread_and_keep(n_heads, rows_per_head):
        col = np.arange(page * n_heads)
        spread = (np.arange(page)[:, None] == col[None, :] // n_heads).astype(np.float32)
        keep = (np.arange(V7X_LANES)[:, None] // rows_per_head == col[None, :] % n_heads).astype(np.float32)
        return jnp.asarray(spread, dtype=BF16), jnp.asarray(keep, dtype=BF16)

    e_d, m_d = spread_and_keep(n_heads_d, 2 * n_q)
    e_s, m_s = spread_and_keep(n_groups, n_q)
    k_shape, vd_shape = (page * n_groups, DIFF_HEAD_DIM), (page * n_heads_d, DIFF_V_DIM)
    vec = fixed((1, DIFF_HEAD_DIM))
    in_specs = ([vec, vec, vec, vec, fixed((1, DIFF_V_DIM)), fixed((page + n_new, V7X_LANES)),
                 fixed((page, page)), fixed(e_d.shape), fixed(m_d.shape), fixed(e_s.shape), fixed(m_s.shape),
                 per_seq(n_q), per_seq(n_q),
                 per_seq(n_new), per_seq(n_new), per_seq(n_new), per_seq(n_new),
                 in_hbm, in_hbm, in_hbm, in_hbm])
    ut = jnp.asarray((np.arange(page)[None, :] > np.arange(page)[:, None]).astype(np.float32), dtype=BF16)
    args = [*lam_vecs, subln_w.reshape(1, DIFF_V_DIM).astype(F32),
            _sample_tail_bias(rel_bias, n_q, n_new, past_len), ut, e_d, m_d, e_s, m_s,
            qd, qs, pad_new(knd), pad_new(kns), pad_new(vnd), pad_new(vns),
            cache_kd, cache_ks, cache_vd, cache_vs]
    assert _PAGE_SLOTS - 1 <= n_steps and n_pages % pps == 0
    kern = functools.partial(_sample_attn_kernel, pages_per_step=pps, n_pages=n_pages, n_q=n_q,
                             n_new=n_new, lam_init=lam_init)
    n_keys = past_len + n_new
    grid_spec = pltpu.PrefetchScalarGridSpec(
        num_scalar_prefetch=1, grid=(bsz,), in_specs=in_specs,
        out_specs=[pl.BlockSpec((None, n_q, w), lambda b, pt: (b, 0, 0))] * 2,
        scratch_shapes=[pltpu.VMEM((2, V7X_LANES, w), BF16),
                        pltpu.VMEM((2, n_keys, V7X_LANES), F32),
                        pltpu.VMEM((2, n_pages + 1, V7X_LANES, page), BF16),
                        pltpu.VMEM((V7X_LANES, DIFF_V_DIM), F32),
                        pltpu.VMEM((V7X_LANES, SB_HEAD_DIM), F32),
                        pltpu.VMEM((_PAGE_SLOTS, 2 * pps) + k_shape, F32),
                        pltpu.VMEM((_PAGE_SLOTS, pps) + vd_shape, F32),
                        pltpu.SemaphoreType.DMA((_PAGE_SLOTS,))])
    return pl.pallas_call(
        kern, grid_spec=grid_spec,
        out_shape=[jax.ShapeDtypeStruct((bsz, n_q, w), F32)] * 2,
        compiler_params=_cparams(("arbitrary",)), name="sample_attn",
    )(page_table, *args)


def _cross_attn_prompt_kernel(q_ref, k_ref, v_ref, o_ref):
    s = _nt_dot(q_ref[...], k_ref[...])
    p = jnp.exp2(s - jnp.max(s, axis=1, keepdims=True))
    o = _dot(p.astype(BF16), v_ref[...]) / jnp.sum(p, axis=1, keepdims=True)
    o_ref[...] = o.astype(o_ref.dtype)


def _cross_attention_prompt(q, mk, mv, *, tq=1024):
    b, t, d = q.shape
    mem_len = mk.shape[1]
    hd = d // MEM_HEADS
    tq = _pick_tile(t, tq)
    q_spec = pl.BlockSpec((None, tq, hd), lambda bi, qi, hi: (bi, qi, hi))
    m_spec = pl.BlockSpec((None, mem_len, hd), lambda bi, qi, hi: (bi, 0, hi))
    return pl.pallas_call(
        _cross_attn_prompt_kernel, grid=(b, t // tq, MEM_HEADS), in_specs=[q_spec, m_spec, m_spec],
        out_specs=q_spec, out_shape=jax.ShapeDtypeStruct((b, t, d), BF16),
        compiler_params=_cparams(("parallel", "parallel", "parallel")), name="cross_attn_prompt",
    )(q, mk, mv)


def _cross_attn_sample_kernel(q_ref, k_ref, v_ref, o_ref, *, n_q):
    mem_len, n_heads, hd = k_ref.shape
    rows, lanes = mem_len * n_heads, V7X_LANES
    q = q_ref[...]
    q_by_head = jnp.concatenate([q[:, h * hd:(h + 1) * hd] for h in range(n_heads)]
                                + [jnp.zeros((lanes - n_heads * n_q, hd), F32)], axis=0)
    k = k_ref[...].reshape(rows, hd).astype(BF16)
    s = _nt_dot(k, q_by_head.astype(BF16))
    row_head = lax.broadcasted_iota(jnp.int32, (rows, lanes), 0) & (n_heads - 1)
    lane = lax.broadcasted_iota(jnp.int32, (rows, lanes), 1)
    s = jnp.where(row_head == (lane >> _log2(n_q)), s, NEG_INF)
    p = jnp.exp2(s - jnp.max(s, axis=0, keepdims=True))
    weights = jnp.transpose(p / jnp.sum(p, axis=0, keepdims=True)).astype(BF16)
    o = _dot(weights, v_ref[...].reshape(rows, hd).astype(BF16))
    for h in range(n_heads):
        o_ref[:, h * hd:(h + 1) * hd] = o[h * n_q:(h + 1) * n_q].astype(o_ref.dtype)


def _cross_attention_sample(q, mem_k, mem_v):
    b, n_q, d = q.shape
    mem_len = mem_k.shape[1]
    q_spec = pl.BlockSpec((None, n_q, d), lambda bi: (bi, 0, 0))
    m_spec = pl.BlockSpec((None, mem_len, MEM_HEADS, d // MEM_HEADS), lambda bi: (bi, 0, 0, 0))
    return pl.pallas_call(
        functools.partial(_cross_attn_sample_kernel, n_q=n_q), grid=(b,),
        in_specs=[q_spec, m_spec, m_spec], out_specs=q_spec,
        out_shape=jax.ShapeDtypeStruct((b, n_q, d), F32),
        compiler_params=_cparams(("parallel",)), name="cross_attn_sample",
    )(q, mem_k, mem_v)


def _weight_matmul(x, w, rounded, key, **kw):
    if rounded is None:
        return _matmul(x, w, emit_w=True, **kw)
    kw.pop("cols", None)
    return [*_matmul(x, rounded[key], **kw), None]


def _post_attention(x, mixed, cross_fn, wts, alpha, q_dtype, rounded):
    (w_out, ln1_g, ln1_b, w_cq, w_co, ln2_g, ln2_b, w_up, b_up, w_down, b_down, ln3_g, ln3_b) = wts
    d = x.shape[1]
    wb = {}
    y, wb["out"] = _weight_matmul(mixed, w_out, rounded, "out", out_dtypes=[F32], res=x, alpha=alpha,
                                  name="mm_out")
    x1, x1b = _layer_norm(y, ln1_g, ln1_b, name="ln1")
    qc, wb["cq"] = _weight_matmul(x1b, w_cq, rounded, "cq", out_dtypes=[q_dtype],
                                  out_scale=(d // MEM_HEADS) ** -0.5 * LOG2E, name="mm_cq")
    oc = cross_fn(qc).astype(BF16)
    y, wb["co"] = _weight_matmul(oc, w_co, rounded, "co", out_dtypes=[F32], res=x1, alpha=alpha, name="mm_co")
    x2, x2b = _layer_norm(y, ln2_g, ln2_b, name="ln2")
    hid, wb["up"] = _weight_matmul(x2b, w_up, rounded, "up", out_dtypes=[BF16], bias=b_up, act="relu2",
                                   name="mm_up")
    y, wb["down"] = _weight_matmul(hid, w_down, rounded, "down", out_dtypes=[F32], bias=b_down, res=x2,
                                   alpha=alpha, tk=2048, name="mm_down")
    (x3,) = _layer_norm(y, ln3_g, ln3_b, out_dtypes=(F32,), name="ln3")
    return x3, wb


_QKV_SECTIONS = ("qd", "kd", "vd", "qs", "ks", "vs")


def _project_qkv(xb, w_in, q_scale, q_dtype, rounded):
    dw = w_in.shape[1] // len(_QKV_SECTIONS)
    res, wb = {}, {}
    for i, sec in enumerate(_QKV_SECTIONS):
        kw = (dict(out_dtypes=[q_dtype], out_scale=q_scale) if sec[0] == "q" else dict(out_dtypes=[F32, BF16]))
        *res[sec], wb[sec] = _weight_matmul(xb, w_in, rounded, sec, cols=(i * dw, dw), name="mm_" + sec, **kw)
    return res, wb


def kernel(x_prompt, x_sample, mem_prompt, cache_diff_k, cache_diff_v, cache_sb_k, cache_sb_v, cache_mem_k, cache_mem_v, page_table, w_in, lambda_q1, lambda_k1, lambda_q2, lambda_k2, subln_w, rel_bias, w_out, ln1_g, ln1_b, w_cq, w_ck, w_cv, w_co, ln2_g, ln2_b, w_up, b_up, w_down, b_down, ln3_g, ln3_b):
    depth = w_in.shape[0]
    assert depth == 1, "single-layer step"
    bsz, seq, d = x_prompt.shape
    dec_b, dec_t, _ = x_sample.shape
    mem_len = mem_prompt.shape[1]
    n_pool, page = cache_diff_k.shape[1], cache_diff_k.shape[2]
    dw = d // 2
    alpha = (2.0 * depth) ** 0.25
    l = 0
    lam_init = 0.8 - 0.6 * math.exp(-0.3 * l)
    q_scale = DIFF_HEAD_DIM ** -0.5 * LOG2E
    assert DIFF_HEAD_DIM == SB_HEAD_DIM

    tail_w = (w_out[l], ln1_g[l], ln1_b[l], w_cq[l], w_co[l], ln2_g[l], ln2_b[l], w_up[l], b_up[l],
              w_down[l], b_down[l], ln3_g[l], ln3_b[l])
    lam_vecs = [v[l].reshape(1, DIFF_HEAD_DIM).astype(F32) for v in (lambda_q1, lambda_k1, lambda_q2, lambda_k2)]
    vnear = _near_bias_log2(rel_bias)
    vtab = jnp.concatenate([vnear[:, :1], vnear[:, :0:-1]], axis=1)

    xs = x_sample.reshape(dec_b * dec_t, d)
    sp, w_in_b = _project_qkv(xs.astype(BF16), w_in[l], q_scale, F32, None)
    s3 = (dec_b, dec_t, dw)
    smix_d, smix_s = _sample_attention(
        sp["qd"][0].reshape(s3), sp["qs"][0].reshape(s3), sp["kd"][1].reshape(s3), sp["ks"][1].reshape(s3),
        sp["vd"][1].reshape(s3), sp["vs"][1].reshape(s3),
        cache_diff_k[l].reshape(n_pool, -1, DIFF_HEAD_DIM), cache_sb_k[l].reshape(n_pool, -1, SB_HEAD_DIM),
        cache_diff_v[l].reshape(n_pool, -1, DIFF_V_DIM), cache_sb_v[l].reshape(n_pool, -1, SB_HEAD_DIM),
        page_table, lam_vecs, subln_w[l], rel_bias, lam_init=lam_init)
    smixed = jnp.concatenate([smix_d, smix_s], axis=-1).reshape(dec_b * dec_t, d).astype(BF16)

    def cross_sample(qc):
        o = _cross_attention_sample(qc.reshape(dec_b, dec_t, d), cache_mem_k[l], cache_mem_v[l])
        return o.reshape(dec_b * dec_t, d)

    y_sample, tail_b = _post_attention(xs, smixed, cross_sample, tail_w, alpha, F32, None)
    y_sample = y_sample.reshape(dec_b, dec_t, d)
    skd32, svd32, sks32, svs32 = (sp[sec][0] for sec in ("kd", "vd", "ks", "vs"))

    xp = x_prompt.reshape(bsz * seq, d)
    pp, _ = _project_qkv(xp.astype(BF16), w_in[l], q_scale, BF16, w_in_b)
    shape3 = (bsz, seq, dw)
    mixed_d = _diff_attention_prompt(pp["qd"][0].reshape(shape3), pp["kd"][1].reshape(shape3),
                                     pp["vd"][1].reshape(shape3), lam_vecs, vtab, subln_w[l], lam_init=lam_init)
    mixed_s = _sb_attention_prompt(pp["qs"][0].reshape(shape3), pp["ks"][1].reshape(shape3),
                                   pp["vs"][1].reshape(shape3))
    mixed = jnp.concatenate([mixed_d, mixed_s], axis=-1).reshape(bsz * seq, d)
    memb = mem_prompt.reshape(bsz * mem_len, d).astype(BF16)
    mk32, mk = _matmul(memb, w_ck[l], out_dtypes=[F32, BF16], name="mm_ck")
    mv32, mv = _matmul(memb, w_cv[l], out_dtypes=[F32, BF16], name="mm_cv")

    def cross_prompt(qc):
        o = _cross_attention_prompt(qc.reshape(bsz, seq, d), mk.reshape(bsz, mem_len, d),
                                    mv.reshape(bsz, mem_len, d))
        return o.reshape(bsz * seq, d)

    y_prompt, _ = _post_attention(xp, mixed, cross_prompt, tail_w, alpha, BF16, tail_b)
    y_prompt = y_prompt.reshape(bsz, seq, d)
    kd32, vd32, ks32, vs32 = (pp[sec][0] for sec in ("kd", "vd", "ks", "vs"))

    dh, hd = DIFF_HEAD_DIM, dw // DIFF_V_DIM
    hs = dw // SB_HEAD_DIM
    mh = d // MEM_HEADS
    return (y_prompt, y_sample,
            kd32.reshape(1, bsz, seq, hd, 2, dh), vd32.reshape(1, bsz, seq, hd, 2 * dh),
            ks32.reshape(1, bsz, seq, hs, SB_HEAD_DIM), vs32.reshape(1, bsz, seq, hs, SB_HEAD_DIM),
            mk32.reshape(1, bsz, mem_len, MEM_HEADS, mh), mv32.reshape(1, bsz, mem_len, MEM_HEADS, mh),
            skd32.reshape(1, dec_b, dec_t, hd, 2, dh), svd32.reshape(1, dec_b, dec_t, hd, 2 * dh),
            sks32.reshape(1, dec_b, dec_t, hs, SB_HEAD_DIM), svs32.reshape(1, dec_b, dec_t, hs, SB_HEAD_DIM))
```
